```python
import math
import jax, jax.numpy as jnp
from jax import lax
import numpy as np

D_MODEL = 1024
BATCH = 16
SEQ = 4096
DEPTH = 1

CTX_LEN = 256
GRID_W = 64
MLA_HEADS = 8
NOPE_DIM = 64
ROPE_DIM = 32
V_DIM = 64
QK_DIM = NOPE_DIM + ROPE_DIM
Q_LORA = 256
KV_LORA = 128
MLA_WIDTH = MLA_HEADS * V_DIM
F_GROUPS = 4
F_GROUP_DIM = 128
F_WIDTH = F_GROUPS * F_GROUP_DIM
D_MIX = MLA_WIDTH + F_WIDTH
IN_SPLITS = (Q_LORA, KV_LORA, ROPE_DIM, MLA_WIDTH, F_WIDTH, F_WIDTH)
D_IN = sum(IN_SPLITS)
ROPE_BASE = 10000.0
Q_BLOCK = 128
LN_EPS = 1e-6
DEEPNORM_ALPHA = (2.0 * DEPTH) ** 0.25
DEEPNORM_BETA = (8.0 * DEPTH) ** -0.25

kernel_name = "hybrid_mla_fnet_prefix_block"


def _layer_norm(x, g=None, b=None):
    xf = x.astype(jnp.float32)
    mu = jnp.mean(xf, axis=-1, keepdims=True)
    var = jnp.mean(jnp.square(xf - mu), axis=-1, keepdims=True)
    y = (xf - mu) * lax.rsqrt(var + LN_EPS)
    if g is not None:
        y = y * g.astype(jnp.float32) + b.astype(jnp.float32)
    return y.astype(x.dtype)


def _rms_norm(x, g):
    xf = x.astype(jnp.float32)
    y = xf * lax.rsqrt(jnp.mean(jnp.square(xf), axis=-1, keepdims=True) + LN_EPS)
    return (y * g.astype(jnp.float32)).astype(x.dtype)


def _modulation(cvec, w_ada, b_ada):
    m = jax.nn.silu(cvec) @ w_ada + b_ada
    return jnp.split(m, 3, axis=-1)


def _axial_rope_tables(n_tokens, dtype):
    n_rows = n_tokens // GRID_W
    rows, cols = jnp.meshgrid(jnp.arange(n_rows), jnp.arange(GRID_W), indexing="ij")
    rows = rows.reshape(-1).astype(jnp.float32)
    cols = cols.reshape(-1).astype(jnp.float32)
    axis_dim = ROPE_DIM // 2
    inv_freq = ROPE_BASE ** (-jnp.arange(0, axis_dim, 2, dtype=jnp.float32) / axis_dim)
    ang = jnp.concatenate([rows[:, None] * inv_freq, cols[:, None] * inv_freq], axis=-1)
    ang = jnp.concatenate([ang, ang], axis=-1)
    return jnp.cos(ang).astype(dtype), jnp.sin(ang).astype(dtype)


def _apply_rope(x, cos, sin):
    half = x.shape[-1] // 2
    rot = jnp.concatenate([-x[..., half:], x[..., :half]], axis=-1)
    return x * cos + rot * sin


def _split_proj(h, w_in, b_in):
    proj = h @ w_in + b_in
    idx = [int(v) for v in np.cumsum(IN_SPLITS)[:-1]]
    return jnp.split(proj, idx, axis=-1)


def _mla_q(q_lat, q_norm_g, w_q_up):
    b, t, _ = q_lat.shape
    q = (_rms_norm(q_lat, q_norm_g) @ w_q_up).reshape(b, t, MLA_HEADS, QK_DIM)
    return q[..., :NOPE_DIM], q[..., NOPE_DIM:]


def _mla_kv(c_kv, kv_norm_g, w_kv_up):
    b, t, _ = c_kv.shape
    kv = (_rms_norm(c_kv, kv_norm_g) @ w_kv_up).reshape(b, t, MLA_HEADS, NOPE_DIM + V_DIM)
    return kv[..., :NOPE_DIM], kv[..., NOPE_DIM:]


def _assemble_k(k_nope, k_rope):
    b, t, h, _ = k_nope.shape
    return jnp.concatenate([k_nope, jnp.broadcast_to(k_rope[:, :, None, :], (b, t, h, ROPE_DIM))], axis=-1)


def _softmax_attend(q, k, v):
    s = jnp.einsum("bqhd,bkhd->bhqk", q, k).astype(jnp.float32) * (1.0 / math.sqrt(QK_DIM))
    p = jax.nn.softmax(s, axis=-1).astype(v.dtype)
    return jnp.einsum("bhqk,bkhv->bqhv", p, v)


def _latent_attention(q, k_lat, v_lat, k_ctx, v_ctx):
    b, s, h, d = q.shape
    k = jnp.concatenate([k_ctx, k_lat], axis=1)
    v = jnp.concatenate([v_ctx, v_lat], axis=1)
    qb = q.reshape(b, s // Q_BLOCK, Q_BLOCK, h, d).transpose(1, 0, 2, 3, 4)
    o = lax.map(lambda qi: _softmax_attend(qi, k, v), qb)
    return o.transpose(1, 0, 2, 3, 4).reshape(b, s, h * V_DIM)


def _fourier_mix(u, w_fourier, b_fourier):
    b, t, _ = u.shape
    ug = u.reshape(b, t, F_GROUPS, F_GROUP_DIM).astype(jnp.float32)
    z = jnp.fft.fft2(ug, axes=(1, 3), norm="ortho").real.astype(u.dtype).reshape(b, t, F_WIDTH)
    return z @ w_fourier + b_fourier


def _merge_out(attn, four, g_mla, g_f, w_out, b_out):
    y = jnp.concatenate([attn * jax.nn.silu(g_mla), four * jax.nn.silu(g_f)], axis=-1)
    return y @ w_out + b_out


def setup_inputs(seed: int = 0) -> dict:
    key = jax.random.key(seed)
    ks = jax.random.split(key, 20)
    f32 = jnp.float32
    nrm = lambda k, shape, s: jax.random.normal(k, shape, f32) * s
    return {
        "x": nrm(ks[0], (BATCH, SEQ, D_MODEL), 1.0),
        "c": nrm(ks[1], (BATCH, D_MODEL), 1.0),
        "ctx": nrm(ks[2], (BATCH, CTX_LEN, D_MODEL), 1.0),
        "c_ctx": nrm(ks[3], (D_MODEL,), 1.0),
        "w_ada": nrm(ks[4], (DEPTH, D_MODEL, 3 * D_MODEL), 0.5 * D_MODEL ** -0.5),
        "b_ada": nrm(ks[5], (DEPTH, 3 * D_MODEL), 0.02),
        "w_in": nrm(ks[6], (DEPTH, D_MODEL, D_IN), D_MODEL ** -0.5),
        "b_in": nrm(ks[7], (DEPTH, D_IN), 0.02),
        "q_norm_g": 1.0 + nrm(ks[8], (DEPTH, Q_LORA), 0.02),
        "w_q_up": nrm(ks[9], (DEPTH, Q_LORA, MLA_HEADS * QK_DIM), Q_LORA ** -0.5),
        "kv_norm_g": 1.0 + nrm(ks[10], (DEPTH, KV_LORA), 0.02),
        "w_kv_up": nrm(ks[11], (DEPTH, KV_LORA, MLA_HEADS * (NOPE_DIM + V_DIM)), KV_LORA ** -0.5),
        "w_fourier": nrm(ks[12], (DEPTH, F_WIDTH, F_WIDTH), F_WIDTH ** -0.5),
        "b_fourier": nrm(ks[13], (DEPTH, F_WIDTH), 0.02),
        "w_out": nrm(ks[14], (DEPTH, D_MIX, D_MODEL), DEEPNORM_BETA * D_MIX ** -0.5),
        "b_out": nrm(ks[15], (DEPTH, D_MODEL), 0.02),
        "post_ln_g": 1.0 + nrm(ks[16], (DEPTH, D_MODEL), 0.02),
        "post_ln_b": nrm(ks[17], (DEPTH, D_MODEL), 0.02),
    }


def reference(x, c, ctx, c_ctx, w_ada, b_ada, w_in, b_in, q_norm_g, w_q_up, kv_norm_g,
              w_kv_up, w_fourier, b_fourier, w_out, b_out, post_ln_g, post_ln_b):
    seq = x.shape[1]
    cos, sin = _axial_rope_tables(seq, x.dtype)
    cos_q, sin_q = cos[None, :, None, :], sin[None, :, None, :]
    cos_k, sin_k = cos[None], sin[None]
    for l in range(DEPTH):
        shift_x, scale_x, gate_x = _modulation(c, w_ada[l], b_ada[l])
        shift_c, scale_c, gate_c = _modulation(c_ctx, w_ada[l], b_ada[l])

        h_c = _layer_norm(ctx) * (1.0 + scale_c) + shift_c
        qlat_c, ckv_c, krope_c, gmla_c, fin_c, gf_c = _split_proj(h_c, w_in[l], b_in[l])
        knope_c, v_c = _mla_kv(ckv_c, kv_norm_g[l], w_kv_up[l])
        k_c = _assemble_k(knope_c, krope_c)

        h_x = _layer_norm(x) * (1.0 + scale_x[:, None, :]) + shift_x[:, None, :]
        qlat_x, ckv_x, krope_x, gmla_x, fin_x, gf_x = _split_proj(h_x, w_in[l], b_in[l])
        qnope_x, qrope_x = _mla_q(qlat_x, q_norm_g[l], w_q_up[l])
        q_x = jnp.concatenate([qnope_x, _apply_rope(qrope_x, cos_q, sin_q)], axis=-1)
        knope_x, v_x = _mla_kv(ckv_x, kv_norm_g[l], w_kv_up[l])
        k_x = _assemble_k(knope_x, _apply_rope(krope_x, cos_k, sin_k))
        attn_x = _latent_attention(q_x, k_x, v_x, k_c, v_c)
        four_x = _fourier_mix(fin_x, w_fourier[l], b_fourier[l])
        y_x = _merge_out(attn_x, four_x, gmla_x, gf_x, w_out[l], b_out[l])
        x_new = _layer_norm(DEEPNORM_ALPHA * x + gate_x[:, None, :] * y_x, post_ln_g[l], post_ln_b[l])

        if l + 1 < DEPTH:
            qnope_c, qrope_c = _mla_q(qlat_c, q_norm_g[l], w_q_up[l])
            q_c = jnp.concatenate([qnope_c, qrope_c], axis=-1)
            b, t, _ = ctx.shape
            attn_c = _softmax_attend(q_c, k_c, v_c).reshape(b, t, MLA_WIDTH)
            four_c = _fourier_mix(fin_c, w_fourier[l], b_fourier[l])
            y_c = _merge_out(attn_c, four_c, gmla_c, gf_c, w_out[l], b_out[l])
            ctx = _layer_norm(DEEPNORM_ALPHA * ctx + gate_c * y_c, post_ln_g[l], post_ln_b[l])
        x = x_new
    return x
```

```python
import functools
import math

import jax
import jax.numpy as jnp
import numpy as np
from jax import lax
from jax.experimental import pallas as pl
from jax.experimental.pallas import tpu as pltpu

D_MODEL = 1024
CTX_LEN = 256
GRID_W = 64
MLA_HEADS = 8
NOPE_DIM = 64
ROPE_DIM = 32
V_DIM = 64
QK_DIM = NOPE_DIM + ROPE_DIM
Q_LORA = 256
KV_LORA = 128
MLA_WIDTH = MLA_HEADS * V_DIM
F_GROUPS = 4
F_GROUP_DIM = 128
F_WIDTH = F_GROUPS * F_GROUP_DIM
D_MIX = MLA_WIDTH + F_WIDTH
ROPE_BASE = 10000.0
LN_EPS = 1e-6
DEPTH = 1
DEEPNORM_ALPHA = (2.0 * DEPTH) ** 0.25

LANES = 128
HEAD_PAD = 128
FFT_RADIX = 64
X_PITCH = 72
Y_PITCH = 136

COL_QLAT = 0
COL_CKV = 256
COL_KROPE = 384
COL_KROT = 512
COL_GMLA = 640
COL_FIN = 1152
COL_GF = 1664
D_IN_PAD = 2176

BF16 = jnp.bfloat16
F32 = jnp.float32

VMEM_LIMIT = 56 * 1024 * 1024


def _dot(a, b):
    return jnp.dot(a, b, preferred_element_type=F32)


def _dot_nt(a, b):
    return lax.dot_general(a, b, (((1,), (1,)), ((), ())), preferred_element_type=F32)


def _layer_norm_rows(x):
    mu = jnp.mean(x, axis=-1, keepdims=True)
    xc = x - mu
    var = jnp.mean(xc * xc, axis=-1, keepdims=True)
    return xc * lax.rsqrt(var + LN_EPS)


def _rms_rows(x, g):
    return x * lax.rsqrt(jnp.mean(x * x, axis=-1, keepdims=True) + LN_EPS) * g


def _silu(x):
    return x * jax.nn.sigmoid(x)


def _mod_kernel(c_ref, w_ref, b_ref, o_ref):
    a = _silu(c_ref[...]).astype(BF16)
    o_ref[...] = _dot(a, w_ref[...]) + b_ref[...]


def _modulation(cvec, w_ada, b_ada):
    rows = cvec.shape[0]
    n = w_ada.shape[1]
    tn = 1024
    return pl.pallas_call(
        _mod_kernel,
        grid=(n // tn,),
        in_specs=[pl.BlockSpec((rows, D_MODEL), lambda j: (0, 0)),
                  pl.BlockSpec((D_MODEL, tn), lambda j: (0, j)),
                  pl.BlockSpec((1, tn), lambda j: (0, j))],
        out_specs=pl.BlockSpec((rows, tn), lambda j: (0, j)),
        out_shape=jax.ShapeDtypeStruct((rows, n), F32),
        name="modulation",
    )(cvec, w_ada, b_ada)


def _ctx_kernel(x_ref, mod_ref, w_ref, b_ref, kvg_ref, wk2_ref, wvt_ref, k_ref, vt_ref):
    y = _layer_norm_rows(x_ref[0])
    h = (y * (1.0 + mod_ref[0, 1:2, :]) + mod_ref[0, 0:1, :]).astype(BF16)
    proj = _dot(h, w_ref[...]) + b_ref[...]
    ckv_n = _rms_rows(proj[:, :KV_LORA], kvg_ref[...])
    ckv_b = ckv_n.astype(BF16)
    kin = jnp.concatenate([ckv_b, proj[:, KV_LORA:].astype(BF16)], axis=1)
    k_ref[0] = _dot(kin, wk2_ref[...]).astype(BF16)
    vt_ref[0] = _dot_nt(wvt_ref[...], ckv_b).astype(BF16)


def _ctx_proj(ctx, mod, mod_row, w_ctx, b_ctx, kvg, wk2, wvt):
    b, t, _ = ctx.shape
    const = lambda *shape: pl.BlockSpec(shape, lambda i: (0,) * len(shape))
    return pl.pallas_call(
        _ctx_kernel,
        grid=(b,),
        in_specs=[pl.BlockSpec((1, t, D_MODEL), lambda i: (i, 0, 0)),
                  pl.BlockSpec((1, 3, D_MODEL), lambda i: (mod_row, 0, 0)),
                  const(D_MODEL, 2 * LANES), const(1, 2 * LANES), const(1, KV_LORA),
                  const(2 * LANES, MLA_HEADS * HEAD_PAD), const(MLA_WIDTH, KV_LORA)],
        out_specs=[pl.BlockSpec((1, t, MLA_HEADS * HEAD_PAD), lambda i: (i, 0, 0)),
                   pl.BlockSpec((1, MLA_WIDTH, t), lambda i: (i, 0, 0))],
        out_shape=[jax.ShapeDtypeStruct((b, t, MLA_HEADS * HEAD_PAD), BF16),
                   jax.ShapeDtypeStruct((b, MLA_WIDTH, t), BF16)],
        compiler_params=pltpu.CompilerParams(vmem_limit_bytes=VMEM_LIMIT),
        name="ctx_proj",
    )(ctx, mod, w_ctx, b_ctx, kvg, wk2, wvt)


def _lat_kernel(x_ref, mod_ref, w_in_ref, b_in_ref, qg_ref, wqt_ref, kvg_ref, wk2_ref, wvt_ref, wc_ref,
                cosq_ref, sinq_ref, cosk_ref, sink_ref,
                qt_ref, k_ref, vt_ref, gates_ref, d_ref, h_scr):
    y = _layer_norm_rows(x_ref[0])
    h_scr[...] = (y * (1.0 + mod_ref[0, 1:2, :]) + mod_ref[0, 0:1, :]).astype(BF16)

    def proj(lo, hi):
        return _dot(h_scr[...], w_in_ref[:, lo:hi]) + b_in_ref[:, lo:hi]

    qn = _rms_rows(proj(COL_QLAT, COL_CKV), qg_ref[...]).astype(BF16)
    qt2 = _dot_nt(wqt_ref[...], qn)
    scale = 1.0 / math.sqrt(QK_DIM)
    cosq = cosq_ref[...]
    sinq = sinq_ref[...]
    rot_base = MLA_HEADS * HEAD_PAD
    for hd in range(MLA_HEADS):
        base = hd * HEAD_PAD
        qt_ref[0, base:base + NOPE_DIM, :] = (qt2[base:base + NOPE_DIM] * scale).astype(BF16)
        rope = (qt2[base + NOPE_DIM:base + QK_DIM] * cosq
                + qt2[rot_base + hd * ROPE_DIM:rot_base + (hd + 1) * ROPE_DIM] * sinq)
        qt_ref[0, base + NOPE_DIM:base + QK_DIM, :] = rope.astype(BF16)
        qt_ref[0, base + QK_DIM:base + HEAD_PAD, :] = jnp.zeros((HEAD_PAD - QK_DIM, qt2.shape[1]), BF16)

    ckv_b = _rms_rows(proj(COL_CKV, COL_KROPE), kvg_ref[...]).astype(BF16)
    kro = proj(COL_KROPE, COL_KROT) * cosk_ref[...] + proj(COL_KROT, COL_GMLA) * sink_ref[...]
    kin = jnp.concatenate([ckv_b, kro.astype(BF16)], axis=1)
    k_ref[0] = _dot(kin, wk2_ref[...]).astype(BF16)
    vt_ref[0] = _dot_nt(wvt_ref[...], ckv_b).astype(BF16)

    gates_ref[0, :, :MLA_WIDTH] = _silu(proj(COL_GMLA, COL_FIN)).astype(BF16)
    gates_ref[0, :, MLA_WIDTH:] = _silu(proj(COL_GF, D_IN_PAD)).astype(BF16)

    d_ref[0] = _dot(proj(COL_FIN, COL_GF).astype(BF16), wc_ref[...]).astype(BF16)


def _lat_proj(x, mod, w_in_p, b_in_p, qg, wqt, kvg, wk2, wvt, wc, cosq, sinq, cosk, sink, tm):
    b, s, _ = x.shape
    const = lambda *shape: pl.BlockSpec(shape, lambda i, j: (0,) * len(shape))
    hp = MLA_HEADS * HEAD_PAD
    return pl.pallas_call(
        _lat_kernel,
        grid=(b, s // tm),
        in_specs=[pl.BlockSpec((1, tm, D_MODEL), lambda i, j: (i, j, 0)),
                  pl.BlockSpec((1, 3, D_MODEL), lambda i, j: (i, 0, 0)),
                  const(D_MODEL, D_IN_PAD), const(1, D_IN_PAD), const(1, Q_LORA),
                  const(hp + MLA_HEADS * ROPE_DIM, Q_LORA), const(1, KV_LORA),
                  const(2 * LANES, hp), const(MLA_WIDTH, KV_LORA), const(F_WIDTH, 2 * F_WIDTH),
                  pl.BlockSpec((ROPE_DIM, tm), lambda i, j: (0, j)),
                  pl.BlockSpec((ROPE_DIM, tm), lambda i, j: (0, j)),
                  pl.BlockSpec((tm, LANES), lambda i, j: (j, 0)),
                  pl.BlockSpec((tm, LANES), lambda i, j: (j, 0))],
        out_specs=[pl.BlockSpec((1, hp, tm), lambda i, j: (i, 0, j)),
                   pl.BlockSpec((1, tm, hp), lambda i, j: (i, j, 0)),
                   pl.BlockSpec((1, MLA_WIDTH, tm), lambda i, j: (i, 0, j)),
                   pl.BlockSpec((1, tm, D_MIX), lambda i, j: (i, j, 0)),
                   pl.BlockSpec((1, tm, 2 * F_WIDTH), lambda i, j: (i, j, 0))],
        out_shape=[jax.ShapeDtypeStruct((b, hp, s), BF16),
                   jax.ShapeDtypeStruct((b, s, hp), BF16),
                   jax.ShapeDtypeStruct((b, MLA_WIDTH, s), BF16),
                   jax.ShapeDtypeStruct((b, s, D_MIX), BF16),
                   jax.ShapeDtypeStruct((b, s, 2 * F_WIDTH), BF16)],
        scratch_shapes=[pltpu.VMEM((tm, D_MODEL), BF16)],
        compiler_params=pltpu.CompilerParams(vmem_limit_bytes=VMEM_LIMIT),
        name="lat_proj",
    )(x, mod, w_in_p, b_in_p, qg, wqt, kvg, wk2, wvt, wc, cosq, sinq, cosk, sink)


def _fft_kernel(dr_ref, di_ref, w1_ref, m3_ref, z_ref, xs, ys, zs):
    r = FFT_RADIX

    def fill(t1, carry):
        src = pl.multiple_of(t1 * r, r)
        dst = pl.multiple_of(t1 * X_PITCH, 8)
        re = dr_ref[0, pl.ds(src, r), :].astype(F32)
        im = di_ref[0, pl.ds(src, r), :].astype(F32)
        xs[0, pl.ds(dst, r), :] = re[:, :LANES]
        xs[1, pl.ds(dst, r), :] = re[:, LANES:]
        xs[2, pl.ds(dst, r), :] = im[:, :LANES]
        xs[3, pl.ds(dst, r), :] = im[:, LANES:]
        return carry

    lax.fori_loop(0, r, fill, 0)

    def stage1(t2, carry):
        parts = [xs[sl, pl.ds(t2, r, stride=X_PITCH), :] for sl in range(4)]
        rhs = jnp.concatenate([jnp.concatenate(parts[0:2], axis=1),
                               jnp.concatenate(parts[2:4], axis=1)], axis=0).astype(BF16)
        y = _dot(w1_ref[...], rhs)
        dst = pl.multiple_of(t2 * Y_PITCH, 8)
        ys[0, pl.ds(dst, 2 * r), :] = y[:, :LANES]
        ys[1, pl.ds(dst, 2 * r), :] = y[:, LANES:]
        return carry

    lax.fori_loop(0, r, stage1, 0)

    def stage2(k1, carry):
        yr = jnp.concatenate([ys[sl, pl.ds(k1, r, stride=Y_PITCH), :] for sl in range(2)], axis=1)
        yi = jnp.concatenate([ys[sl, pl.ds(r + k1, r, stride=Y_PITCH), :] for sl in range(2)], axis=1)
        rhs = jnp.concatenate([yr, yi], axis=0).astype(BF16)
        z = _dot(m3_ref[k1], rhs)
        zs[0, pl.ds(k1, r, stride=X_PITCH), :] = z[:, :LANES]
        zs[1, pl.ds(k1, r, stride=X_PITCH), :] = z[:, LANES:]
        return carry

    lax.fori_loop(0, r, stage2, 0)

    def drain(k2, carry):
        src = pl.multiple_of(k2 * X_PITCH, 8)
        dst = pl.multiple_of(k2 * r, r)
        z_ref[0, pl.ds(dst, r), :] = jnp.concatenate(
            [zs[0, pl.ds(src, r), :], zs[1, pl.ds(src, r), :]], axis=1).astype(BF16)
        return carry

    lax.fori_loop(0, r, drain, 0)


def _fft(d, w1, m3):
    b, s, _ = d.shape
    r = FFT_RADIX
    cw = 2 * LANES
    nblk = F_WIDTH // cw
    return pl.pallas_call(
        _fft_kernel,
        grid=(b, nblk),
        in_specs=[pl.BlockSpec((1, s, cw), lambda i, g: (i, 0, g)),
                  pl.BlockSpec((1, s, cw), lambda i, g: (i, 0, nblk + g)),
                  pl.BlockSpec((2 * r, 2 * r), lambda i, g: (0, 0)),
                  pl.BlockSpec((r, r, 2 * r), lambda i, g: (0, 0, 0))],
        out_specs=pl.BlockSpec((1, s, cw), lambda i, g: (i, 0, g)),
        out_shape=jax.ShapeDtypeStruct((b, s, F_WIDTH), BF16),
        scratch_shapes=[pltpu.VMEM((4, r * X_PITCH, LANES), F32),
                        pltpu.VMEM((2, r * Y_PITCH, LANES), F32),
                        pltpu.VMEM((2, r * X_PITCH, LANES), F32)],
        compiler_params=pltpu.CompilerParams(vmem_limit_bytes=VMEM_LIMIT),
        name="fft",
    )(d, d, w1, m3)


def _attn_kernel(qt_ref, kc_ref, kl_ref, vct_ref, vlt_ref, o_ref, s_scr, p_scr, *, ck):
    tq = qt_ref.shape[2]
    n_ctx = kc_ref.shape[1]
    n_lat = kl_ref.shape[1]
    outs = []
    for hd in range(2):
        lanes = slice(hd * HEAD_PAD, (hd + 1) * HEAD_PAD)
        qt = qt_ref[0, lanes, :]

        def colmax8(v):
            return jnp.max(v.reshape(v.shape[0] // 8, 8, tq), axis=0)

        def colsum8(v):
            return jnp.sum(v.reshape(v.shape[0] // 8, 8, tq), axis=0)

        s0 = _dot(kc_ref[0, :, lanes], qt)
        s_scr[0:n_ctx, :] = s0

        def scores(c, m8):
            off = pl.multiple_of(c * ck, ck)
            s = _dot(kl_ref[0, pl.ds(off, ck), lanes], qt)
            s_scr[pl.ds(n_ctx + off, ck), :] = s
            return jnp.maximum(m8, colmax8(s))

        m8 = lax.fori_loop(0, n_lat // ck, scores, colmax8(s0))
        m = jnp.max(m8, axis=0, keepdims=True)

        def probs(c, l8):
            off = pl.multiple_of(c * ck, ck)
            p = jnp.exp(s_scr[pl.ds(off, ck), :] - m)
            p_scr[pl.ds(off, ck), :] = p.astype(BF16)
            return l8 + colsum8(p)

        l8 = lax.fori_loop(0, (n_ctx + n_lat) // ck, probs, jnp.zeros((8, tq), F32))
        l = jnp.sum(l8, axis=0, keepdims=True)

        rows = slice(hd * V_DIM, (hd + 1) * V_DIM)
        ot = (_dot(vct_ref[0, rows, :], p_scr[0:n_ctx, :])
              + _dot(vlt_ref[0, rows, :], p_scr[n_ctx:, :]))
        outs.append(ot / l)
    o_ref[0] = jnp.concatenate(outs, axis=0).T.astype(BF16)


def _attention(qt, kc, kl, vct, vlt, tq, ck):
    b, hp, s = qt.shape
    n_ctx = kc.shape[1]
    pairs = MLA_HEADS // 2
    return pl.pallas_call(
        functools.partial(_attn_kernel, ck=ck),
        grid=(b, pairs, s // tq),
        in_specs=[pl.BlockSpec((1, 2 * HEAD_PAD, tq), lambda i, h, j: (i, h, j)),
                  pl.BlockSpec((1, n_ctx, 2 * HEAD_PAD), lambda i, h, j: (i, 0, h)),
                  pl.BlockSpec((1, s, 2 * HEAD_PAD), lambda i, h, j: (i, 0, h)),
                  pl.BlockSpec((1, 2 * V_DIM, n_ctx), lambda i, h, j: (i, h, 0)),
                  pl.BlockSpec((1, 2 * V_DIM, s), lambda i, h, j: (i, h, 0))],
        out_specs=pl.BlockSpec((1, tq, 2 * V_DIM), lambda i, h, j: (i, j, h)),
        out_shape=jax.ShapeDtypeStruct((b, s, MLA_WIDTH), BF16),
        scratch_shapes=[pltpu.VMEM((n_ctx + s, tq), F32),
                        pltpu.VMEM((n_ctx + s, tq), BF16)],
        compiler_params=pltpu.CompilerParams(vmem_limit_bytes=VMEM_LIMIT),
        name="attention",
    )(qt, kc, kl, vct, vlt)


def _merge_kernel(x_ref, mod_ref, attn_ref, z_ref, gates_ref, wf_ref, bf_ref, wo_ref, bo_ref, g_ref, b_ref, o_ref):
    four = _dot(z_ref[0], wf_ref[...]) + bf_ref[...]
    gates = gates_ref[0]
    ya = attn_ref[0].astype(F32) * gates[:, :MLA_WIDTH].astype(F32)
    yf = four * gates[:, MLA_WIDTH:].astype(F32)
    ycat = jnp.concatenate([ya.astype(BF16), yf.astype(BF16)], axis=1)
    y = _dot(ycat, wo_ref[...]) + bo_ref[...]
    r = DEEPNORM_ALPHA * x_ref[0] + mod_ref[0, 2:3, :] * y
    o_ref[0] = _layer_norm_rows(r) * g_ref[...] + b_ref[...]


def _merge(x, mod, attn, z, gates, wf, bf, wo, bo, g, bb, tm):
    b, s, _ = x.shape
    const = lambda *shape: pl.BlockSpec(shape, lambda i, j: (0,) * len(shape))
    tok = lambda w: pl.BlockSpec((1, tm, w), lambda i, j: (i, j, 0))
    return pl.pallas_call(
        _merge_kernel,
        grid=(b, s // tm),
        in_specs=[tok(D_MODEL), pl.BlockSpec((1, 3, D_MODEL), lambda i, j: (i, 0, 0)),
                  tok(MLA_WIDTH), tok(F_WIDTH), tok(D_MIX),
                  const(F_WIDTH, F_WIDTH), const(1, F_WIDTH), const(D_MIX, D_MODEL), const(1, D_MODEL),
                  const(1, D_MODEL), const(1, D_MODEL)],
        out_specs=tok(D_MODEL),
        out_shape=jax.ShapeDtypeStruct((b, s, D_MODEL), x.dtype),
        compiler_params=pltpu.CompilerParams(vmem_limit_bytes=VMEM_LIMIT),
        name="merge",
    )(x, mod, attn, z, gates, wf, bf, wo, bo, g, bb)


def _rot_cols(w):
    half = w.shape[-1] // 2
    return jnp.concatenate([-w[..., half:], w[..., :half]], axis=-1)


def _rope_tables(s):
    n_rows = s // GRID_W
    pos = np.arange(s)
    rows = (pos // GRID_W).astype(np.float64)
    cols = (pos % GRID_W).astype(np.float64)
    assert n_rows * GRID_W == s
    axis_dim = ROPE_DIM // 2
    inv_freq = ROPE_BASE ** (-np.arange(0, axis_dim, 2, dtype=np.float64) / axis_dim)
    ang = np.concatenate([rows[:, None] * inv_freq, cols[:, None] * inv_freq], axis=-1)
    ang = np.concatenate([ang, ang], axis=-1)
    return np.cos(ang), np.sin(ang)


def _dft_constants():
    r = FFT_RADIX
    n = r * r
    k = np.arange(r, dtype=np.float64)
    ang1 = 2.0 * np.pi * np.outer(k, k) / r
    c1, s1 = np.cos(ang1), np.sin(ang1)
    w1 = np.block([[c1, s1], [-s1, c1]])
    k1 = np.arange(r)[:, None, None]
    k2 = np.arange(r)[None, :, None]
    t2 = np.arange(r)[None, None, :]
    ang3 = 2.0 * np.pi * t2 * (k1 + r * k2) / n
    m3 = np.concatenate([np.cos(ang3), np.sin(ang3)], axis=-1) / r
    c = np.arange(F_GROUP_DIM, dtype=np.float64)
    angc = 2.0 * np.pi * np.outer(c, c) / F_GROUP_DIM
    norm = 1.0 / math.sqrt(F_GROUP_DIM)
    eye = np.eye(F_GROUPS)
    wc = np.concatenate([np.kron(eye, np.cos(angc) * norm), -np.kron(eye, np.sin(angc) * norm)], axis=1)
    return w1, m3, wc


def kernel(x, c, ctx, c_ctx, w_ada, b_ada, w_in, b_in, q_norm_g, w_q_up, kv_norm_g, w_kv_up, w_fourier,
           b_fourier, w_out, b_out, post_ln_g, post_ln_b):
    b, s, _ = x.shape
    assert s == FFT_RADIX * FFT_RADIX and w_ada.shape[0] == 1
    l = 0

    wi, bi = w_in[l], b_in[l]
    o_q, o_kv, o_kr, o_gm, o_f, o_gf = 0, Q_LORA, Q_LORA + KV_LORA, Q_LORA + KV_LORA + ROPE_DIM, \
        Q_LORA + KV_LORA + ROPE_DIM + MLA_WIDTH, Q_LORA + KV_LORA + ROPE_DIM + MLA_WIDTH + F_WIDTH

    def in_cols(a):
        kr = a[..., o_kr:o_gm]
        zpad = jnp.zeros(a.shape[:-1] + (LANES - ROPE_DIM,), a.dtype)
        return jnp.concatenate([a[..., o_q:o_kr], kr, zpad, _rot_cols(kr), zpad, a[..., o_gm:]], axis=-1)

    w_in_p = in_cols(wi).astype(BF16)
    b_in_p = in_cols(bi)[None, :]
    w_ctx = w_in_p[:, COL_CKV:COL_KROT]
    b_ctx = b_in_p[:, COL_CKV:COL_KROT]

    wq = w_q_up[l].reshape(Q_LORA, MLA_HEADS, QK_DIM)
    wq_pad = jnp.pad(wq, ((0, 0), (0, 0), (0, HEAD_PAD - QK_DIM))).reshape(Q_LORA, MLA_HEADS * HEAD_PAD)
    wq_rot = _rot_cols(wq[..., NOPE_DIM:]).reshape(Q_LORA, MLA_HEADS * ROPE_DIM)
    wqt = jnp.concatenate([wq_pad, wq_rot], axis=1).T.astype(BF16)

    wkv = w_kv_up[l].reshape(KV_LORA, MLA_HEADS, NOPE_DIM + V_DIM)
    wk_pad = jnp.pad(wkv[..., :NOPE_DIM], ((0, 0), (0, 0), (0, HEAD_PAD - NOPE_DIM)))
    wk_pad = wk_pad.reshape(KV_LORA, MLA_HEADS * HEAD_PAD)
    place = np.zeros((LANES, MLA_HEADS, HEAD_PAD), np.float32)
    for j in range(ROPE_DIM):
        place[j, :, NOPE_DIM + j] = 1.0
    wk2 = jnp.concatenate([wk_pad, jnp.asarray(place.reshape(LANES, -1))], axis=0).astype(BF16)
    wvt = wkv[..., NOPE_DIM:].reshape(KV_LORA, MLA_WIDTH).T.astype(BF16)

    w1_np, m3_np, wc_np = _dft_constants()
    w1 = jnp.asarray(w1_np, F32).astype(BF16)
    m3 = jnp.asarray(m3_np, F32).astype(BF16)
    wc = jnp.asarray(wc_np, F32).astype(BF16)

    cos_np, sin_np = _rope_tables(s)
    scale = 1.0 / math.sqrt(QK_DIM)
    cosq = jnp.asarray((cos_np * scale).T, F32)
    sinq = jnp.asarray((sin_np * scale).T, F32)
    kpad = np.zeros((s, LANES - ROPE_DIM))
    cosk = jnp.asarray(np.concatenate([cos_np, kpad], axis=1), F32)
    sink = jnp.asarray(np.concatenate([sin_np, kpad], axis=1), F32)

    qg = q_norm_g[l][None, :]
    kvg = kv_norm_g[l][None, :]

    rows = ((b + 1 + 7) // 8) * 8
    cvec = jnp.concatenate([c, c_ctx[None, :], jnp.zeros((rows - b - 1, D_MODEL), c.dtype)], axis=0)
    mod = _modulation(cvec, w_ada[l].astype(BF16), b_ada[l][None, :]).reshape(rows, 3, D_MODEL)

    kc, vct = _ctx_proj(ctx, mod, b, w_ctx, b_ctx, kvg, wk2, wvt)
    qt, kl, vlt, gates, d = _lat_proj(x, mod, w_in_p, b_in_p, qg, wqt, kvg, wk2, wvt, wc,
                                      cosq, sinq, cosk, sink, tm=512)
    z = _fft(d, w1, m3)
    attn = _attention(qt, kc, kl, vct, vlt, tq=512, ck=256)
    return _merge(x, mod, attn, z, gates, w_fourier[l].astype(BF16), b_fourier[l][None, :],
                  w_out[l].astype(BF16), b_out[l][None, :], post_ln_g[l][None, :], post_ln_b[l][None, :], tm=512)
```

```python
import functools
import math

import jax
import jax.numpy as jnp
import numpy as np
from jax import lax
from jax.experimental import pallas as pl
from jax.experimental.pallas import tpu as pltpu

D_MODEL = 1024
CTX_LEN = 256
GRID_W = 64
MLA_HEADS = 8
NOPE_DIM = 64
ROPE_DIM = 32
V_DIM = 64
QK_DIM = NOPE_DIM + ROPE_DIM
Q_LORA = 256
KV_LORA = 128
MLA_WIDTH = MLA_HEADS * V_DIM
F_GROUPS = 4
F_GROUP_DIM = 128
F_WIDTH = F_GROUPS * F_GROUP_DIM
D_MIX = MLA_WIDTH + F_WIDTH
ROPE_BASE = 10000.0
LN_EPS = 1e-6
DEPTH = 1
DEEPNORM_ALPHA = (2.0 * DEPTH) ** 0.25
QK_SCALE = math.log2(math.e) / math.sqrt(QK_DIM)

LANES = 128
HEAD_PAD = 128
FFT_RADIX = 64
X_PITCH = 72
Y_PITCH = 136

COL_QLAT = 0
COL_CKV = 256
COL_KROPE = 384
COL_KROT = 512
COL_GMLA = 640
COL_FIN = 1152
COL_GF = 1664
D_IN_PAD = 2176

BF16 = jnp.bfloat16
F32 = jnp.float32

VMEM_LIMIT = 56 * 1024 * 1024


def _dot(a, b):
    return jnp.dot(a, b, preferred_element_type=F32)


def _dot_nt(a, b):
    return lax.dot_general(a, b, (((1,), (1,)), ((), ())), preferred_element_type=F32)


def _layer_norm_rows(x):
    mu = jnp.mean(x, axis=-1, keepdims=True)
    xc = x - mu
    var = jnp.mean(xc * xc, axis=-1, keepdims=True)
    return xc * lax.rsqrt(var + LN_EPS)


def _rms_rows(x, g):
    return x * lax.rsqrt(jnp.mean(x * x, axis=-1, keepdims=True) + LN_EPS) * g


def _silu(x):
    return x * jax.nn.sigmoid(x)


def _mod_kernel(c_ref, w_ref, b_ref, o_ref):
    a = _silu(c_ref[...]).astype(BF16)
    o_ref[...] = _dot(a, w_ref[...]) + b_ref[...]


def _modulation(cvec, w_ada, b_ada):
    rows = cvec.shape[0]
    n = w_ada.shape[1]
    tn = 1024
    return pl.pallas_call(
        _mod_kernel,
        grid=(n // tn,),
        in_specs=[pl.BlockSpec((rows, D_MODEL), lambda j: (0, 0)),
                  pl.BlockSpec((D_MODEL, tn), lambda j: (0, j)),
                  pl.BlockSpec((1, tn), lambda j: (0, j))],
        out_specs=pl.BlockSpec((rows, tn), lambda j: (0, j)),
        out_shape=jax.ShapeDtypeStruct((rows, n), F32),
        name="modulation",
    )(cvec, w_ada, b_ada)


def _ctx_kernel(x_ref, mod_ref, w_ref, b_ref, kvg_ref, wk2_ref, wvt_ref, k_ref, vt_ref):
    y = _layer_norm_rows(x_ref[0])
    h = (y * (1.0 + mod_ref[0, 1:2, :]) + mod_ref[0, 0:1, :]).astype(BF16)
    proj = _dot(h, w_ref[...]) + b_ref[...]
    ckv_n = _rms_rows(proj[:, :KV_LORA], kvg_ref[...])
    ckv_b = ckv_n.astype(BF16)
    kin = jnp.concatenate([ckv_b, proj[:, KV_LORA:].astype(BF16)], axis=1)
    k_ref[0] = _dot(kin, wk2_ref[...]).astype(BF16)
    vt_ref[0] = _dot_nt(wvt_ref[...], ckv_b).astype(BF16)


def _ctx_proj(ctx, mod, mod_row, w_ctx, b_ctx, kvg, wk2, wvt):
    b, t, _ = ctx.shape
    const = lambda *shape: pl.BlockSpec(shape, lambda i: (0,) * len(shape))
    return pl.pallas_call(
        _ctx_kernel,
        grid=(b,),
        in_specs=[pl.BlockSpec((1, t, D_MODEL), lambda i: (i, 0, 0)),
                  pl.BlockSpec((1, 3, D_MODEL), lambda i: (mod_row, 0, 0)),
                  const(D_MODEL, 2 * LANES), const(1, 2 * LANES), const(1, KV_LORA),
                  const(2 * LANES, MLA_HEADS * HEAD_PAD), const(MLA_WIDTH, KV_LORA)],
        out_specs=[pl.BlockSpec((1, t, MLA_HEADS * HEAD_PAD), lambda i: (i, 0, 0)),
                   pl.BlockSpec((1, MLA_WIDTH, t), lambda i: (i, 0, 0))],
        out_shape=[jax.ShapeDtypeStruct((b, t, MLA_HEADS * HEAD_PAD), BF16),
                   jax.ShapeDtypeStruct((b, MLA_WIDTH, t), BF16)],
        compiler_params=pltpu.CompilerParams(vmem_limit_bytes=VMEM_LIMIT),
        name="ctx_proj",
    )(ctx, mod, w_ctx, b_ctx, kvg, wk2, wvt)


def _lat_kernel(x_ref, mod_ref, w_in_ref, b_in_ref, qg_ref, wqt_ref, kvg_ref, wk2_ref, wvt_ref, wc_ref,
                cosq_ref, sinq_ref, cosk_ref, sink_ref,
                qt_ref, k_ref, vt_ref, gates_ref, d_ref, h_scr):
    y = _layer_norm_rows(x_ref[0])
    h_scr[...] = (y * (1.0 + mod_ref[0, 1:2, :]) + mod_ref[0, 0:1, :]).astype(BF16)

    def proj(lo, hi):
        return _dot(h_scr[...], w_in_ref[:, lo:hi]) + b_in_ref[:, lo:hi]

    qn = _rms_rows(proj(COL_QLAT, COL_CKV), qg_ref[...]).astype(BF16)
    qt2 = _dot_nt(wqt_ref[...], qn)
    scale = QK_SCALE
    cosq = cosq_ref[...]
    sinq = sinq_ref[...]
    rot_base = MLA_HEADS * HEAD_PAD
    for hd in range(MLA_HEADS):
        base = hd * HEAD_PAD
        qt_ref[0, base:base + NOPE_DIM, :] = (qt2[base:base + NOPE_DIM] * scale).astype(BF16)
        rope = (qt2[base + NOPE_DIM:base + QK_DIM] * cosq
                + qt2[rot_base + hd * ROPE_DIM:rot_base + (hd + 1) * ROPE_DIM] * sinq)
        qt_ref[0, base + NOPE_DIM:base + QK_DIM, :] = rope.astype(BF16)
        qt_ref[0, base + QK_DIM:base + HEAD_PAD, :] = jnp.zeros((HEAD_PAD - QK_DIM, qt2.shape[1]), BF16)

    ckv_b = _rms_rows(proj(COL_CKV, COL_KROPE), kvg_ref[...]).astype(BF16)
    kro = proj(COL_KROPE, COL_KROT) * cosk_ref[...] + proj(COL_KROT, COL_GMLA) * sink_ref[...]
    kin = jnp.concatenate([ckv_b, kro.astype(BF16)], axis=1)
    k_ref[0] = _dot(kin, wk2_ref[...]).astype(BF16)
    vt_ref[0] = _dot_nt(wvt_ref[...], ckv_b).astype(BF16)

    gates_ref[0, :, :MLA_WIDTH] = _silu(proj(COL_GMLA, COL_FIN)).astype(BF16)
    gates_ref[0, :, MLA_WIDTH:] = _silu(proj(COL_GF, D_IN_PAD)).astype(BF16)

    d_ref[0] = _dot(proj(COL_FIN, COL_GF).astype(BF16), wc_ref[...]).astype(BF16)


def _lat_proj(x, mod, w_in_p, b_in_p, qg, wqt, kvg, wk2, wvt, wc, cosq, sinq, cosk, sink, tm):
    b, s, _ = x.shape
    const = lambda *shape: pl.BlockSpec(shape, lambda i, j: (0,) * len(shape))
    hp = MLA_HEADS * HEAD_PAD
    return pl.pallas_call(
        _lat_kernel,
        grid=(b, s // tm),
        in_specs=[pl.BlockSpec((1, tm, D_MODEL), lambda i, j: (i, j, 0)),
                  pl.BlockSpec((1, 3, D_MODEL), lambda i, j: (i, 0, 0)),
                  const(D_MODEL, D_IN_PAD), const(1, D_IN_PAD), const(1, Q_LORA),
                  const(hp + MLA_HEADS * ROPE_DIM, Q_LORA), const(1, KV_LORA),
                  const(2 * LANES, hp), const(MLA_WIDTH, KV_LORA), const(F_WIDTH, 2 * F_WIDTH),
                  pl.BlockSpec((ROPE_DIM, tm), lambda i, j: (0, j)),
                  pl.BlockSpec((ROPE_DIM, tm), lambda i, j: (0, j)),
                  pl.BlockSpec((tm, LANES), lambda i, j: (j, 0)),
                  pl.BlockSpec((tm, LANES), lambda i, j: (j, 0))],
        out_specs=[pl.BlockSpec((1, hp, tm), lambda i, j: (i, 0, j)),
                   pl.BlockSpec((1, tm, hp), lambda i, j: (i, j, 0)),
                   pl.BlockSpec((1, MLA_WIDTH, tm), lambda i, j: (i, 0, j)),
                   pl.BlockSpec((1, tm, D_MIX), lambda i, j: (i, j, 0)),
                   pl.BlockSpec((1, tm, 2 * F_WIDTH), lambda i, j: (i, j, 0))],
        out_shape=[jax.ShapeDtypeStruct((b, hp, s), BF16),
                   jax.ShapeDtypeStruct((b, s, hp), BF16),
                   jax.ShapeDtypeStruct((b, MLA_WIDTH, s), BF16),
                   jax.ShapeDtypeStruct((b, s, D_MIX), BF16),
                   jax.ShapeDtypeStruct((b, s, 2 * F_WIDTH), BF16)],
        scratch_shapes=[pltpu.VMEM((tm, D_MODEL), BF16)],
        compiler_params=pltpu.CompilerParams(vmem_limit_bytes=VMEM_LIMIT),
        name="lat_proj",
    )(x, mod, w_in_p, b_in_p, qg, wqt, kvg, wk2, wvt, wc, cosq, sinq, cosk, sink)


def _fft_kernel(dr_ref, di_ref, w1_ref, m3_ref, z_ref, xs, ys, zs):
    r = FFT_RADIX

    def fill(t1, carry):
        src = pl.multiple_of(t1 * r, r)
        dst = pl.multiple_of(t1 * X_PITCH, 8)
        re = dr_ref[0, pl.ds(src, r), :].astype(F32)
        im = di_ref[0, pl.ds(src, r), :].astype(F32)
        xs[0, pl.ds(dst, r), :] = re[:, :LANES]
        xs[1, pl.ds(dst, r), :] = re[:, LANES:]
        xs[2, pl.ds(dst, r), :] = im[:, :LANES]
        xs[3, pl.ds(dst, r), :] = im[:, LANES:]
        return carry

    lax.fori_loop(0, r, fill, 0)

    def stage1(t2, carry):
        parts = [xs[sl, pl.ds(t2, r, stride=X_PITCH), :] for sl in range(4)]
        rhs = jnp.concatenate([jnp.concatenate(parts[0:2], axis=1),
                               jnp.concatenate(parts[2:4], axis=1)], axis=0).astype(BF16)
        y = _dot(w1_ref[...], rhs)
        dst = pl.multiple_of(t2 * Y_PITCH, 8)
        ys[0, pl.ds(dst, 2 * r), :] = y[:, :LANES]
        ys[1, pl.ds(dst, 2 * r), :] = y[:, LANES:]
        return carry

    lax.fori_loop(0, r, stage1, 0)

    def stage2(k1, carry):
        yr = jnp.concatenate([ys[sl, pl.ds(k1, r, stride=Y_PITCH), :] for sl in range(2)], axis=1)
        yi = jnp.concatenate([ys[sl, pl.ds(r + k1, r, stride=Y_PITCH), :] for sl in range(2)], axis=1)
        rhs = jnp.concatenate([yr, yi], axis=0).astype(BF16)
        z = _dot(m3_ref[k1], rhs)
        zs[0, pl.ds(k1, r, stride=X_PITCH), :] = z[:, :LANES]
        zs[1, pl.ds(k1, r, stride=X_PITCH), :] = z[:, LANES:]
        return carry

    lax.fori_loop(0, r, stage2, 0)

    def drain(k2, carry):
        src = pl.multiple_of(k2 * X_PITCH, 8)
        dst = pl.multiple_of(k2 * r, r)
        z_ref[0, pl.ds(dst, r), :] = jnp.concatenate(
            [zs[0, pl.ds(src, r), :], zs[1, pl.ds(src, r), :]], axis=1).astype(BF16)
        return carry

    lax.fori_loop(0, r, drain, 0)


def _fft(d, w1, m3):
    b, s, _ = d.shape
    r = FFT_RADIX
    cw = 2 * LANES
    nblk = F_WIDTH // cw
    return pl.pallas_call(
        _fft_kernel,
        grid=(b, nblk),
        in_specs=[pl.BlockSpec((1, s, cw), lambda i, g: (i, 0, g)),
                  pl.BlockSpec((1, s, cw), lambda i, g: (i, 0, nblk + g)),
                  pl.BlockSpec((2 * r, 2 * r), lambda i, g: (0, 0)),
                  pl.BlockSpec((r, r, 2 * r), lambda i, g: (0, 0, 0))],
        out_specs=pl.BlockSpec((1, s, cw), lambda i, g: (i, 0, g)),
        out_shape=jax.ShapeDtypeStruct((b, s, F_WIDTH), BF16),
        scratch_shapes=[pltpu.VMEM((4, r * X_PITCH, LANES), F32),
                        pltpu.VMEM((2, r * Y_PITCH, LANES), F32),
                        pltpu.VMEM((2, r * X_PITCH, LANES), F32)],
        compiler_params=pltpu.CompilerParams(vmem_limit_bytes=VMEM_LIMIT),
        name="fft",
    )(d, d, w1, m3)


def _attn_kernel(qt_ref, kc_ref, kl_ref, vct_ref, vlt_ref, o_ref, s_scr, p_scr, acc_scr, l_scr, o_scr, *, ck, sck):
    tq = qt_ref.shape[2]
    n_ctx = kc_ref.shape[1]
    n_lat = kl_ref.shape[1]
    nh = MLA_HEADS

    def fold8(v, op):
        return op(v.reshape(v.shape[0] // 8, 8, tq), axis=0)

    m8 = None
    for ph in range(nh + 2):
        h_qk = ph if ph < nh else None
        h_ex = ph - 1 if 0 <= ph - 1 < nh else None
        h_pv = ph - 2 if 0 <= ph - 2 < nh else None
        if h_ex is not None:
            mb = jnp.broadcast_to(jnp.max(m8, axis=0, keepdims=True), (8, tq))

        def work(k_ref, v_ref, src, row0, n, m8c, l8c, first):
            rows = pl.ds(row0, n)
            if h_ex is not None:
                sv = s_scr[h_ex % 2, rows, :]
                p = jnp.exp2(sv.reshape(n // 8, 8, tq) - mb[None]).reshape(n, tq)
                p_scr[h_ex % 2, rows, :] = p.astype(BF16)
                l8c = l8c + fold8(p, jnp.sum)
            if h_qk is not None:
                lanes = slice(h_qk * HEAD_PAD, (h_qk + 1) * HEAD_PAD)
                s = _dot(k_ref[0, pl.ds(src, n), lanes], qt_ref[0, lanes, :])
                s_scr[h_qk % 2, rows, :] = s
                m8c = jnp.maximum(m8c, fold8(s, jnp.max))
            if h_pv is not None:
                vrows = slice(h_pv * V_DIM, (h_pv + 1) * V_DIM)
                o = _dot(v_ref[0, vrows, pl.ds(src, n)], p_scr[h_pv % 2, rows, :])
                if first:
                    acc_scr[...] = o
                else:
                    acc_scr[...] += o
            return m8c, l8c

        neg = jnp.full((8, tq), -jnp.inf, F32)
        zero = jnp.zeros((8, tq), F32)
        carry = work(kc_ref, vct_ref, 0, 0, n_ctx, neg, zero, True)

        def body(c, carry):
            for sub in range(ck // sck):
                off = pl.multiple_of(c * ck + sub * sck, sck)
                carry = work(kl_ref, vlt_ref, off, n_ctx + off, sck, carry[0], carry[1], False)
            return carry

        m8n, l8 = lax.fori_loop(0, n_lat // ck, body, carry)
        if h_pv is not None:
            o_scr[h_pv * V_DIM:(h_pv + 1) * V_DIM, :] = acc_scr[...] / l_scr[h_pv % 2]
        if h_ex is not None:
            l_scr[h_ex % 2] = jnp.broadcast_to(jnp.sum(l8, axis=0, keepdims=True), (V_DIM, tq))
        m8 = m8n
    o_ref[0] = o_scr[...].T.astype(BF16)


def _attention(qt, kc, kl, vct, vlt, tq, ck, sck):
    b, hp, s = qt.shape
    n_ctx = kc.shape[1]
    once = dict(pipeline_mode=pl.Buffered(1))
    return pl.pallas_call(
        functools.partial(_attn_kernel, ck=ck, sck=sck),
        grid=(b, s // tq),
        in_specs=[pl.BlockSpec((1, hp, tq), lambda i, j: (i, 0, j)),
                  pl.BlockSpec((1, n_ctx, hp), lambda i, j: (i, 0, 0), **once),
                  pl.BlockSpec((1, s, hp), lambda i, j: (i, 0, 0), **once),
                  pl.BlockSpec((1, MLA_WIDTH, n_ctx), lambda i, j: (i, 0, 0), **once),
                  pl.BlockSpec((1, MLA_WIDTH, s), lambda i, j: (i, 0, 0), **once)],
        out_specs=pl.BlockSpec((1, tq, MLA_WIDTH), lambda i, j: (i, j, 0)),
        out_shape=jax.ShapeDtypeStruct((b, s, MLA_WIDTH), BF16),
        scratch_shapes=[pltpu.VMEM((2, n_ctx + s, tq), F32),
                        pltpu.VMEM((2, n_ctx + s, tq), BF16),
                        pltpu.VMEM((V_DIM, tq), F32),
                        pltpu.VMEM((2, V_DIM, tq), F32),
                        pltpu.VMEM((MLA_WIDTH, tq), F32)],
        compiler_params=pltpu.CompilerParams(vmem_limit_bytes=VMEM_LIMIT),
        name="attention",
    )(qt, kc, kl, vct, vlt)


def _merge_kernel(x_ref, mod_ref, attn_ref, z_ref, gates_ref, wf_ref, bf_ref, wo_ref, bo_ref, g_ref, b_ref, o_ref):
    four = _dot(z_ref[0], wf_ref[...]) + bf_ref[...]
    gates = gates_ref[0]
    ya = attn_ref[0].astype(F32) * gates[:, :MLA_WIDTH].astype(F32)
    yf = four * gates[:, MLA_WIDTH:].astype(F32)
    ycat = jnp.concatenate([ya.astype(BF16), yf.astype(BF16)], axis=1)
    y = _dot(ycat, wo_ref[...]) + bo_ref[...]
    r = DEEPNORM_ALPHA * x_ref[0] + mod_ref[0, 2:3, :] * y
    o_ref[0] = _layer_norm_rows(r) * g_ref[...] + b_ref[...]


def _merge(x, mod, attn, z, gates, wf, bf, wo, bo, g, bb, tm):
    b, s, _ = x.shape
    const = lambda *shape: pl.BlockSpec(shape, lambda i, j: (0,) * len(shape))
    tok = lambda w: pl.BlockSpec((1, tm, w), lambda i, j: (i, j, 0))
    return pl.pallas_call(
        _merge_kernel,
        grid=(b, s // tm),
        in_specs=[tok(D_MODEL), pl.BlockSpec((1, 3, D_MODEL), lambda i, j: (i, 0, 0)),
                  tok(MLA_WIDTH), tok(F_WIDTH), tok(D_MIX),
                  const(F_WIDTH, F_WIDTH), const(1, F_WIDTH), const(D_MIX, D_MODEL), const(1, D_MODEL),
                  const(1, D_MODEL), const(1, D_MODEL)],
        out_specs=tok(D_MODEL),
        out_shape=jax.ShapeDtypeStruct((b, s, D_MODEL), x.dtype),
        compiler_params=pltpu.CompilerParams(vmem_limit_bytes=VMEM_LIMIT),
        name="merge",
    )(x, mod, attn, z, gates, wf, bf, wo, bo, g, bb)


def _rot_cols(w):
    half = w.shape[-1] // 2
    return jnp.concatenate([-w[..., half:], w[..., :half]], axis=-1)


def _rope_tables(s):
    n_rows = s // GRID_W
    pos = np.arange(s)
    rows = (pos // GRID_W).astype(np.float64)
    cols = (pos % GRID_W).astype(np.float64)
    assert n_rows * GRID_W == s
    axis_dim = ROPE_DIM // 2
    inv_freq = ROPE_BASE ** (-np.arange(0, axis_dim, 2, dtype=np.float64) / axis_dim)
    ang = np.concatenate([rows[:, None] * inv_freq, cols[:, None] * inv_freq], axis=-1)
    ang = np.concatenate([ang, ang], axis=-1)
    return np.cos(ang), np.sin(ang)


def _dft_constants():
    r = FFT_RADIX
    n = r * r
    k = np.arange(r, dtype=np.float64)
    ang1 = 2.0 * np.pi * np.outer(k, k) / r
    c1, s1 = np.cos(ang1), np.sin(ang1)
    w1 = np.block([[c1, s1], [-s1, c1]])
    k1 = np.arange(r)[:, None, None]
    k2 = np.arange(r)[None, :, None]
    t2 = np.arange(r)[None, None, :]
    ang3 = 2.0 * np.pi * t2 * (k1 + r * k2) / n
    m3 = np.concatenate([np.cos(ang3), np.sin(ang3)], axis=-1) / r
    c = np.arange(F_GROUP_DIM, dtype=np.float64)
    angc = 2.0 * np.pi * np.outer(c, c) / F_GROUP_DIM
    norm = 1.0 / math.sqrt(F_GROUP_DIM)
    eye = np.eye(F_GROUPS)
    wc = np.concatenate([np.kron(eye, np.cos(angc) * norm), -np.kron(eye, np.sin(angc) * norm)], axis=1)
    return w1, m3, wc


def kernel(x, c, ctx, c_ctx, w_ada, b_ada, w_in, b_in, q_norm_g, w_q_up, kv_norm_g, w_kv_up, w_fourier,
           b_fourier, w_out, b_out, post_ln_g, post_ln_b):
    b, s, _ = x.shape
    assert s == FFT_RADIX * FFT_RADIX and w_ada.shape[0] == 1
    l = 0

    wi, bi = w_in[l], b_in[l]
    o_q, o_kv, o_kr, o_gm, o_f, o_gf = 0, Q_LORA, Q_LORA + KV_LORA, Q_LORA + KV_LORA + ROPE_DIM, \
        Q_LORA + KV_LORA + ROPE_DIM + MLA_WIDTH, Q_LORA + KV_LORA + ROPE_DIM + MLA_WIDTH + F_WIDTH

    def in_cols(a):
        kr = a[..., o_kr:o_gm]
        zpad = jnp.zeros(a.shape[:-1] + (LANES - ROPE_DIM,), a.dtype)
        return jnp.concatenate([a[..., o_q:o_kr], kr, zpad, _rot_cols(kr), zpad, a[..., o_gm:]], axis=-1)

    w_in_p = in_cols(wi).astype(BF16)
    b_in_p = in_cols(bi)[None, :]
    w_ctx = w_in_p[:, COL_CKV:COL_KROT]
    b_ctx = b_in_p[:, COL_CKV:COL_KROT]

    wq = w_q_up[l].reshape(Q_LORA, MLA_HEADS, QK_DIM)
    wq_pad = jnp.pad(wq, ((0, 0), (0, 0), (0, HEAD_PAD - QK_DIM))).reshape(Q_LORA, MLA_HEADS * HEAD_PAD)
    wq_rot = _rot_cols(wq[..., NOPE_DIM:]).reshape(Q_LORA, MLA_HEADS * ROPE_DIM)
    wqt = jnp.concatenate([wq_pad, wq_rot], axis=1).T.astype(BF16)

    wkv = w_kv_up[l].reshape(KV_LORA, MLA_HEADS, NOPE_DIM + V_DIM)
    wk_pad = jnp.pad(wkv[..., :NOPE_DIM], ((0, 0), (0, 0), (0, HEAD_PAD - NOPE_DIM)))
    wk_pad = wk_pad.reshape(KV_LORA, MLA_HEADS * HEAD_PAD)
    place = np.zeros((LANES, MLA_HEADS, HEAD_PAD), np.float32)
    for j in range(ROPE_DIM):
        place[j, :, NOPE_DIM + j] = 1.0
    wk2 = jnp.concatenate([wk_pad, jnp.asarray(place.reshape(LANES, -1))], axis=0).astype(BF16)
    wvt = wkv[..., NOPE_DIM:].reshape(KV_LORA, MLA_WIDTH).T.astype(BF16)

    w1_np, m3_np, wc_np = _dft_constants()
    w1 = jnp.asarray(w1_np, F32).astype(BF16)
    m3 = jnp.asarray(m3_np, F32).astype(BF16)
    wc = jnp.asarray(wc_np, F32).astype(BF16)

    cos_np, sin_np = _rope_tables(s)
    scale = QK_SCALE
    cosq = jnp.asarray((cos_np * scale).T, F32)
    sinq = jnp.asarray((sin_np * scale).T, F32)
    kpad = np.zeros((s, LANES - ROPE_DIM))
    cosk = jnp.asarray(np.concatenate([cos_np, kpad], axis=1), F32)
    sink = jnp.asarray(np.concatenate([sin_np, kpad], axis=1), F32)

    qg = q_norm_g[l][None, :]
    kvg = kv_norm_g[l][None, :]

    rows = ((b + 1 + 7) // 8) * 8
    cvec = jnp.concatenate([c, c_ctx[None, :], jnp.zeros((rows - b - 1, D_MODEL), c.dtype)], axis=0)
    mod = _modulation(cvec, w_ada[l].astype(BF16), b_ada[l][None, :]).reshape(rows, 3, D_MODEL)

    kc, vct = _ctx_proj(ctx, mod, b, w_ctx, b_ctx, kvg, wk2, wvt)
    qt, kl, vlt, gates, d = _lat_proj(x, mod, w_in_p, b_in_p, qg, wqt, kvg, wk2, wvt, wc,
                                      cosq, sinq, cosk, sink, tm=512)
    z = _fft(d, w1, m3)
    attn = _attention(qt, kc, kl, vct, vlt, tq=512, ck=2048, sck=256)
    return _merge(x, mod, attn, z, gates, w_fourier[l].astype(BF16), b_fourier[l][None, :],
                  w_out[l].astype(BF16), b_out[l][None, :], post_ln_g[l][None, :], post_ln_b[l][None, :], tm=512)
```

```python
import functools
import math

import jax
import jax.numpy as jnp
import numpy as np
from jax import lax
from jax.experimental import pallas as pl
from jax.experimental.pallas import tpu as pltpu

D_MODEL = 1024
CTX_LEN = 256
GRID_W = 64
MLA_HEADS = 8
NOPE_DIM = 64
ROPE_DIM = 32
V_DIM = 64
QK_DIM = NOPE_DIM + ROPE_DIM
Q_LORA = 256
KV_LORA = 128
MLA_WIDTH = MLA_HEADS * V_DIM
F_GROUPS = 4
F_GROUP_DIM = 128
F_WIDTH = F_GROUPS * F_GROUP_DIM
D_MIX = MLA_WIDTH + F_WIDTH
ROPE_BASE = 10000.0
LN_EPS = 1e-6
DEPTH = 1
DEEPNORM_ALPHA = (2.0 * DEPTH) ** 0.25
QK_SCALE = math.log2(math.e) / math.sqrt(QK_DIM)

LANES = 128
HEAD_PAD = 128
FFT_RADIX = 64
X_PITCH = 72
Y_PITCH = 136
FFT_UNROLL = 8

COL_QLAT = 0
COL_CKV = 256
COL_KROPE = 384
COL_GMLA = 512
COL_FIN = 1024
COL_GF = 1536
D_IN_PAD = 2048

BF16 = jnp.bfloat16
F32 = jnp.float32

VMEM_LIMIT = 56 * 1024 * 1024


def _dot(a, b):
    return jnp.dot(a, b, preferred_element_type=F32)


def _dot_nt(a, b):
    return lax.dot_general(a, b, (((1,), (1,)), ((), ())), preferred_element_type=F32)


def _layer_norm_rows(x):
    mu = jnp.mean(x, axis=-1, keepdims=True)
    xc = x - mu
    var = jnp.mean(xc * xc, axis=-1, keepdims=True)
    return xc * lax.rsqrt(var + LN_EPS)


def _rms_rows(x, g):
    return x * lax.rsqrt(jnp.mean(x * x, axis=-1, keepdims=True) + LN_EPS) * g


def _silu(x):
    return x * jax.nn.sigmoid(x)


def _mod_kernel(c_ref, w_ref, b_ref, o_ref):
    a = _silu(c_ref[...]).astype(BF16)
    o_ref[...] = _dot(a, w_ref[...]) + b_ref[...]


def _modulation(cvec, w_ada, b_ada):
    rows = cvec.shape[0]
    n = w_ada.shape[1]
    tn = 1024
    return pl.pallas_call(
        _mod_kernel,
        grid=(n // tn,),
        in_specs=[pl.BlockSpec((rows, D_MODEL), lambda j: (0, 0)),
                  pl.BlockSpec((D_MODEL, tn), lambda j: (0, j)),
                  pl.BlockSpec((1, tn), lambda j: (0, j))],
        out_specs=pl.BlockSpec((rows, tn), lambda j: (0, j)),
        out_shape=jax.ShapeDtypeStruct((rows, n), F32),
        name="modulation",
    )(cvec, w_ada, b_ada)


def _ctx_kernel(x_ref, mod_ref, w_ref, b_ref, kvg_ref, wk2_ref, wvt_ref, k_ref, vt_ref):
    y = _layer_norm_rows(x_ref[0])
    h = (y * (1.0 + mod_ref[0, 1:2, :]) + mod_ref[0, 0:1, :]).astype(BF16)
    proj = _dot(h, w_ref[...]) + b_ref[...]
    ckv_n = _rms_rows(proj[:, :KV_LORA], kvg_ref[...])
    ckv_b = ckv_n.astype(BF16)
    kin = jnp.concatenate([ckv_b, proj[:, KV_LORA:].astype(BF16)], axis=1)
    k_ref[0] = _dot(kin, wk2_ref[...]).astype(BF16)
    vt_ref[0] = _dot_nt(wvt_ref[...], ckv_b).astype(BF16)


def _ctx_proj(ctx, mod, mod_row, w_ctx, b_ctx, kvg, wk2, wvt):
    b, t, _ = ctx.shape
    const = lambda *shape: pl.BlockSpec(shape, lambda i: (0,) * len(shape))
    return pl.pallas_call(
        _ctx_kernel,
        grid=(b,),
        in_specs=[pl.BlockSpec((1, t, D_MODEL), lambda i: (i, 0, 0)),
                  pl.BlockSpec((1, 3, D_MODEL), lambda i: (mod_row, 0, 0)),
                  const(D_MODEL, 2 * LANES), const(1, 2 * LANES), const(1, KV_LORA),
                  const(2 * LANES, MLA_HEADS * HEAD_PAD), const(MLA_WIDTH, KV_LORA)],
        out_specs=[pl.BlockSpec((1, t, MLA_HEADS * HEAD_PAD), lambda i: (i, 0, 0)),
                   pl.BlockSpec((1, MLA_WIDTH, t), lambda i: (i, 0, 0))],
        out_shape=[jax.ShapeDtypeStruct((b, t, MLA_HEADS * HEAD_PAD), BF16),
                   jax.ShapeDtypeStruct((b, MLA_WIDTH, t), BF16)],
        compiler_params=pltpu.CompilerParams(vmem_limit_bytes=VMEM_LIMIT),
        name="ctx_proj",
    )(ctx, mod, w_ctx, b_ctx, kvg, wk2, wvt)


def _lat_kernel(x_ref, mod_ref, w_in_ref, b_in_ref, qg_ref, wqt_ref, kvg_ref, wk2_ref, wvt_ref, wc_ref,
                cosq_ref, sinq_ref, cosk_ref, sink_ref,
                qt_ref, k_ref, vt_ref, gates_ref, d_ref, h_scr):
    y = _layer_norm_rows(x_ref[0])
    h_scr[...] = (y * (1.0 + mod_ref[0, 1:2, :]) + mod_ref[0, 0:1, :]).astype(BF16)

    def proj(lo, hi):
        return _dot(h_scr[...], w_in_ref[:, lo:hi]) + b_in_ref[:, lo:hi]

    low = proj(COL_QLAT, COL_GMLA)

    qn = _rms_rows(low[:, COL_QLAT:COL_CKV], qg_ref[...]).astype(BF16)
    qt2 = _dot_nt(wqt_ref[...], qn)
    scale = QK_SCALE
    cosq = cosq_ref[...]
    sinq = sinq_ref[...]
    rot_base = MLA_HEADS * HEAD_PAD
    for hd in range(MLA_HEADS):
        base = hd * HEAD_PAD
        qt_ref[0, base:base + NOPE_DIM, :] = (qt2[base:base + NOPE_DIM] * scale).astype(BF16)
        rope = (qt2[base + NOPE_DIM:base + QK_DIM] * cosq
                + qt2[rot_base + hd * ROPE_DIM:rot_base + (hd + 1) * ROPE_DIM] * sinq)
        qt_ref[0, base + NOPE_DIM:base + QK_DIM, :] = rope.astype(BF16)
        qt_ref[0, base + QK_DIM:base + HEAD_PAD, :] = jnp.zeros((HEAD_PAD - QK_DIM, qt2.shape[1]), BF16)

    ckv_b = _rms_rows(low[:, COL_CKV:COL_KROPE], kvg_ref[...]).astype(BF16)
    kr = low[:, COL_KROPE:COL_GMLA]
    half = ROPE_DIM // 2
    rot = pltpu.roll(kr, half, axis=1) - pltpu.roll(kr, LANES - half, axis=1)
    kro = kr * cosk_ref[...] + rot * sink_ref[...]
    kin = jnp.concatenate([ckv_b, kro.astype(BF16)], axis=1)
    k_ref[0] = _dot(kin, wk2_ref[...]).astype(BF16)
    vt_ref[0] = _dot_nt(wvt_ref[...], ckv_b).astype(BF16)

    gates_ref[0, :, :MLA_WIDTH] = _silu(proj(COL_GMLA, COL_FIN)).astype(BF16)
    gates_ref[0, :, MLA_WIDTH:] = _silu(proj(COL_GF, D_IN_PAD)).astype(BF16)

    d_ref[0] = _dot(proj(COL_FIN, COL_GF).astype(BF16), wc_ref[...]).astype(BF16)


def _lat_proj(x, mod, w_in_p, b_in_p, qg, wqt, kvg, wk2, wvt, wc, cosq, sinq, cosk, sink, tm):
    b, s, _ = x.shape
    const = lambda *shape: pl.BlockSpec(shape, lambda i, j: (0,) * len(shape))
    hp = MLA_HEADS * HEAD_PAD
    return pl.pallas_call(
        _lat_kernel,
        grid=(b, s // tm),
        in_specs=[pl.BlockSpec((1, tm, D_MODEL), lambda i, j: (i, j, 0)),
                  pl.BlockSpec((1, 3, D_MODEL), lambda i, j: (i, 0, 0)),
                  const(D_MODEL, D_IN_PAD), const(1, D_IN_PAD), const(1, Q_LORA),
                  const(hp + MLA_HEADS * ROPE_DIM, Q_LORA), const(1, KV_LORA),
                  const(2 * LANES, hp), const(MLA_WIDTH, KV_LORA), const(F_WIDTH, 2 * F_WIDTH),
                  pl.BlockSpec((ROPE_DIM, tm), lambda i, j: (0, j)),
                  pl.BlockSpec((ROPE_DIM, tm), lambda i, j: (0, j)),
                  pl.BlockSpec((tm, LANES), lambda i, j: (j, 0)),
                  pl.BlockSpec((tm, LANES), lambda i, j: (j, 0))],
        out_specs=[pl.BlockSpec((1, hp, tm), lambda i, j: (i, 0, j)),
                   pl.BlockSpec((1, tm, hp), lambda i, j: (i, j, 0)),
                   pl.BlockSpec((1, MLA_WIDTH, tm), lambda i, j: (i, 0, j)),
                   pl.BlockSpec((1, tm, D_MIX), lambda i, j: (i, j, 0)),
                   pl.BlockSpec((1, tm, 2 * F_WIDTH), lambda i, j: (i, j, 0))],
        out_shape=[jax.ShapeDtypeStruct((b, hp, s), BF16),
                   jax.ShapeDtypeStruct((b, s, hp), BF16),
                   jax.ShapeDtypeStruct((b, MLA_WIDTH, s), BF16),
                   jax.ShapeDtypeStruct((b, s, D_MIX), BF16),
                   jax.ShapeDtypeStruct((b, s, 2 * F_WIDTH), BF16)],
        scratch_shapes=[pltpu.VMEM((tm, D_MODEL), BF16)],
        compiler_params=pltpu.CompilerParams(vmem_limit_bytes=VMEM_LIMIT),
        name="lat_proj",
    )(x, mod, w_in_p, b_in_p, qg, wqt, kvg, wk2, wvt, wc, cosq, sinq, cosk, sink)


def _fft_kernel(dr_ref, di_ref, w1_ref, m3_ref, z_ref, xs, ys, zs):
    r = FFT_RADIX

    def fill(t1, carry):
        src = pl.multiple_of(t1 * r, r)
        dst = pl.multiple_of(t1 * X_PITCH, 8)
        re = dr_ref[0, pl.ds(src, r), :].astype(F32)
        im = di_ref[0, pl.ds(src, r), :].astype(F32)
        xs[0, pl.ds(dst, r), :] = re[:, :LANES]
        xs[1, pl.ds(dst, r), :] = re[:, LANES:]
        xs[2, pl.ds(dst, r), :] = im[:, :LANES]
        xs[3, pl.ds(dst, r), :] = im[:, LANES:]
        return carry

    lax.fori_loop(0, r, fill, 0)

    def stage1(g, carry):
        cols = []
        for u in range(FFT_UNROLL):
            t2 = g * FFT_UNROLL + u
            parts = [xs[sl, pl.ds(t2, r, stride=X_PITCH), :] for sl in range(4)]
            cols.append(jnp.concatenate([jnp.concatenate(parts[0:2], axis=1),
                                         jnp.concatenate(parts[2:4], axis=1)], axis=0))
        rhs = jnp.concatenate(cols, axis=1).astype(BF16)
        y = _dot(w1_ref[...], rhs)
        for u in range(FFT_UNROLL):
            dst = pl.multiple_of((g * FFT_UNROLL + u) * Y_PITCH, 8)
            ys[0, pl.ds(dst, 2 * r), :] = y[:, (2 * u) * LANES:(2 * u + 1) * LANES]
            ys[1, pl.ds(dst, 2 * r), :] = y[:, (2 * u + 1) * LANES:(2 * u + 2) * LANES]
        return carry

    lax.fori_loop(0, r // FFT_UNROLL, stage1, 0)

    def stage2(g, carry):
        for u in range(FFT_UNROLL):
            k1 = g * FFT_UNROLL + u
            yr = jnp.concatenate([ys[sl, pl.ds(k1, r, stride=Y_PITCH), :] for sl in range(2)], axis=1)
            yi = jnp.concatenate([ys[sl, pl.ds(r + k1, r, stride=Y_PITCH), :] for sl in range(2)], axis=1)
            rhs = jnp.concatenate([yr, yi], axis=0).astype(BF16)
            z = _dot(m3_ref[k1], rhs)
            zs[0, pl.ds(k1, r, stride=X_PITCH), :] = z[:, :LANES]
            zs[1, pl.ds(k1, r, stride=X_PITCH), :] = z[:, LANES:]
        return carry

    lax.fori_loop(0, r // FFT_UNROLL, stage2, 0)

    def drain(k2, carry):
        src = pl.multiple_of(k2 * X_PITCH, 8)
        dst = pl.multiple_of(k2 * r, r)
        z_ref[0, pl.ds(dst, r), :] = jnp.concatenate(
            [zs[0, pl.ds(src, r), :], zs[1, pl.ds(src, r), :]], axis=1).astype(BF16)
        return carry

    lax.fori_loop(0, r, drain, 0)


def _fft(d, w1, m3):
    b, s, _ = d.shape
    r = FFT_RADIX
    cw = 2 * LANES
    nblk = F_WIDTH // cw
    return pl.pallas_call(
        _fft_kernel,
        grid=(b, nblk),
        in_specs=[pl.BlockSpec((1, s, cw), lambda i, g: (i, 0, g)),
                  pl.BlockSpec((1, s, cw), lambda i, g: (i, 0, nblk + g)),
                  pl.BlockSpec((2 * r, 2 * r), lambda i, g: (0, 0)),
                  pl.BlockSpec((r, r, 2 * r), lambda i, g: (0, 0, 0))],
        out_specs=pl.BlockSpec((1, s, cw), lambda i, g: (i, 0, g)),
        out_shape=jax.ShapeDtypeStruct((b, s, F_WIDTH), BF16),
        scratch_shapes=[pltpu.VMEM((4, r * X_PITCH, LANES), F32),
                        pltpu.VMEM((2, r * Y_PITCH, LANES), F32),
                        pltpu.VMEM((2, r * X_PITCH, LANES), F32)],
        compiler_params=pltpu.CompilerParams(vmem_limit_bytes=VMEM_LIMIT),
        name="fft",
    )(d, d, w1, m3)


def _attn_kernel(qt_ref, kc_ref, kl_ref, vct_ref, vlt_ref, o_ref, s_scr, p_scr, acc_scr, l_scr, o_scr, *, ck, sck):
    tq = qt_ref.shape[2]
    n_ctx = kc_ref.shape[1]
    n_lat = kl_ref.shape[1]
    nh = MLA_HEADS

    def fold8(v, op):
        return op(v.reshape(v.shape[0] // 8, 8, tq), axis=0)

    m8 = None
    for ph in range(nh + 2):
        h_qk = ph if ph < nh else None
        h_ex = ph - 1 if 0 <= ph - 1 < nh else None
        h_pv = ph - 2 if 0 <= ph - 2 < nh else None
        if h_ex is not None:
            mb = jnp.broadcast_to(jnp.max(m8, axis=0, keepdims=True), (8, tq))

        def work(k_ref, v_ref, src, row0, n, m8c, l8c, first):
            rows = pl.ds(row0, n)
            if h_ex is not None:
                sv = s_scr[h_ex % 2, rows, :]
                p = jnp.exp2(sv.reshape(n // 8, 8, tq) - mb[None]).reshape(n, tq)
                p_scr[h_ex % 2, rows, :] = p.astype(BF16)
                l8c = l8c + fold8(p, jnp.sum)
            if h_qk is not None:
                lanes = slice(h_qk * HEAD_PAD, (h_qk + 1) * HEAD_PAD)
                s = _dot(k_ref[0, pl.ds(src, n), lanes], qt_ref[0, lanes, :])
                s_scr[h_qk % 2, rows, :] = s
                m8c = jnp.maximum(m8c, fold8(s, jnp.max))
            if h_pv is not None:
                vrows = slice(h_pv * V_DIM, (h_pv + 1) * V_DIM)
                o = _dot(v_ref[0, vrows, pl.ds(src, n)], p_scr[h_pv % 2, rows, :])
                if first:
                    acc_scr[...] = o
                else:
                    acc_scr[...] += o
            return m8c, l8c

        neg = jnp.full((8, tq), -jnp.inf, F32)
        zero = jnp.zeros((8, tq), F32)
        carry = work(kc_ref, vct_ref, 0, 0, n_ctx, neg, zero, True)

        def body(c, carry):
            for sub in range(ck // sck):
                off = pl.multiple_of(c * ck + sub * sck, sck)
                carry = work(kl_ref, vlt_ref, off, n_ctx + off, sck, carry[0], carry[1], False)
            return carry

        m8n, l8 = lax.fori_loop(0, n_lat // ck, body, carry)
        if h_pv is not None:
            o_scr[h_pv * V_DIM:(h_pv + 1) * V_DIM, :] = acc_scr[...] / l_scr[h_pv % 2]
        if h_ex is not None:
            l_scr[h_ex % 2] = jnp.broadcast_to(jnp.sum(l8, axis=0, keepdims=True), (V_DIM, tq))
        m8 = m8n
    o_ref[0] = o_scr[...].T.astype(BF16)


def _attention(qt, kc, kl, vct, vlt, tq, ck, sck):
    b, hp, s = qt.shape
    n_ctx = kc.shape[1]
    once = dict(pipeline_mode=pl.Buffered(1))
    return pl.pallas_call(
        functools.partial(_attn_kernel, ck=ck, sck=sck),
        grid=(b, s // tq),
        in_specs=[pl.BlockSpec((1, hp, tq), lambda i, j: (i, 0, j)),
                  pl.BlockSpec((1, n_ctx, hp), lambda i, j: (i, 0, 0), **once),
                  pl.BlockSpec((1, s, hp), lambda i, j: (i, 0, 0), **once),
                  pl.BlockSpec((1, MLA_WIDTH, n_ctx), lambda i, j: (i, 0, 0), **once),
                  pl.BlockSpec((1, MLA_WIDTH, s), lambda i, j: (i, 0, 0), **once)],
        out_specs=pl.BlockSpec((1, tq, MLA_WIDTH), lambda i, j: (i, j, 0)),
        out_shape=jax.ShapeDtypeStruct((b, s, MLA_WIDTH), BF16),
        scratch_shapes=[pltpu.VMEM((2, n_ctx + s, tq), F32),
                        pltpu.VMEM((2, n_ctx + s, tq), BF16),
                        pltpu.VMEM((V_DIM, tq), F32),
                        pltpu.VMEM((2, V_DIM, tq), F32),
                        pltpu.VMEM((MLA_WIDTH, tq), F32)],
        compiler_params=pltpu.CompilerParams(vmem_limit_bytes=VMEM_LIMIT),
        name="attention",
    )(qt, kc, kl, vct, vlt)


def _merge_kernel(x_ref, mod_ref, attn_ref, z_ref, gates_ref, wf_ref, bf_ref, wo_ref, bo_ref, g_ref, b_ref, o_ref):
    four = _dot(z_ref[0], wf_ref[...]) + bf_ref[...]
    gates = gates_ref[0]
    ya = attn_ref[0].astype(F32) * gates[:, :MLA_WIDTH].astype(F32)
    yf = four * gates[:, MLA_WIDTH:].astype(F32)
    ycat = jnp.concatenate([ya.astype(BF16), yf.astype(BF16)], axis=1)
    y = _dot(ycat, wo_ref[...]) + bo_ref[...]
    r = DEEPNORM_ALPHA * x_ref[0] + mod_ref[0, 2:3, :] * y
    o_ref[0] = _layer_norm_rows(r) * g_ref[...] + b_ref[...]


def _merge(x, mod, attn, z, gates, wf, bf, wo, bo, g, bb, tm):
    b, s, _ = x.shape
    const = lambda *shape: pl.BlockSpec(shape, lambda i, j: (0,) * len(shape))
    tok = lambda w: pl.BlockSpec((1, tm, w), lambda i, j: (i, j, 0))
    return pl.pallas_call(
        _merge_kernel,
        grid=(b, s // tm),
        in_specs=[tok(D_MODEL), pl.BlockSpec((1, 3, D_MODEL), lambda i, j: (i, 0, 0)),
                  tok(MLA_WIDTH), tok(F_WIDTH), tok(D_MIX),
                  const(F_WIDTH, F_WIDTH), const(1, F_WIDTH), const(D_MIX, D_MODEL), const(1, D_MODEL),
                  const(1, D_MODEL), const(1, D_MODEL)],
        out_specs=tok(D_MODEL),
        out_shape=jax.ShapeDtypeStruct((b, s, D_MODEL), x.dtype),
        compiler_params=pltpu.CompilerParams(vmem_limit_bytes=VMEM_LIMIT),
        name="merge",
    )(x, mod, attn, z, gates, wf, bf, wo, bo, g, bb)


def _rot_cols(w):
    half = w.shape[-1] // 2
    return jnp.concatenate([-w[..., half:], w[..., :half]], axis=-1)


def _rope_tables(s):
    n_rows = s // GRID_W
    pos = np.arange(s)
    rows = (pos // GRID_W).astype(np.float64)
    cols = (pos % GRID_W).astype(np.float64)
    assert n_rows * GRID_W == s
    axis_dim = ROPE_DIM // 2
    inv_freq = ROPE_BASE ** (-np.arange(0, axis_dim, 2, dtype=np.float64) / axis_dim)
    ang = np.concatenate([rows[:, None] * inv_freq, cols[:, None] * inv_freq], axis=-1)
    ang = np.concatenate([ang, ang], axis=-1)
    return np.cos(ang), np.sin(ang)


def _dft_constants():
    r = FFT_RADIX
    n = r * r
    k = np.arange(r, dtype=np.float64)
    ang1 = 2.0 * np.pi * np.outer(k, k) / r
    c1, s1 = np.cos(ang1), np.sin(ang1)
    w1 = np.block([[c1, s1], [-s1, c1]])
    k1 = np.arange(r)[:, None, None]
    k2 = np.arange(r)[None, :, None]
    t2 = np.arange(r)[None, None, :]
    ang3 = 2.0 * np.pi * t2 * (k1 + r * k2) / n
    m3 = np.concatenate([np.cos(ang3), np.sin(ang3)], axis=-1) / r
    c = np.arange(F_GROUP_DIM, dtype=np.float64)
    angc = 2.0 * np.pi * np.outer(c, c) / F_GROUP_DIM
    norm = 1.0 / math.sqrt(F_GROUP_DIM)
    eye = np.eye(F_GROUPS)
    wc = np.concatenate([np.kron(eye, np.cos(angc) * norm), -np.kron(eye, np.sin(angc) * norm)], axis=1)
    return w1, m3, wc


def kernel(x, c, ctx, c_ctx, w_ada, b_ada, w_in, b_in, q_norm_g, w_q_up, kv_norm_g, w_kv_up, w_fourier,
           b_fourier, w_out, b_out, post_ln_g, post_ln_b):
    b, s, _ = x.shape
    assert s == FFT_RADIX * FFT_RADIX and w_ada.shape[0] == 1
    l = 0

    wi, bi = w_in[l], b_in[l]
    o_q, o_kv, o_kr, o_gm, o_f, o_gf = 0, Q_LORA, Q_LORA + KV_LORA, Q_LORA + KV_LORA + ROPE_DIM, \
        Q_LORA + KV_LORA + ROPE_DIM + MLA_WIDTH, Q_LORA + KV_LORA + ROPE_DIM + MLA_WIDTH + F_WIDTH

    def in_cols(a):
        kr = a[..., o_kr:o_gm]
        zpad = jnp.zeros(a.shape[:-1] + (LANES - ROPE_DIM,), a.dtype)
        return jnp.concatenate([a[..., o_q:o_kr], kr, zpad, a[..., o_gm:]], axis=-1)

    w_in_p = in_cols(wi).astype(BF16)
    b_in_p = in_cols(bi)[None, :]
    w_ctx = w_in_p[:, COL_CKV:COL_GMLA]
    b_ctx = b_in_p[:, COL_CKV:COL_GMLA]

    wq = w_q_up[l].reshape(Q_LORA, MLA_HEADS, QK_DIM)
    wq_pad = jnp.pad(wq, ((0, 0), (0, 0), (0, HEAD_PAD - QK_DIM))).reshape(Q_LORA, MLA_HEADS * HEAD_PAD)
    wq_rot = _rot_cols(wq[..., NOPE_DIM:]).reshape(Q_LORA, MLA_HEADS * ROPE_DIM)
    wqt = jnp.concatenate([wq_pad, wq_rot], axis=1).T.astype(BF16)

    wkv = w_kv_up[l].reshape(KV_LORA, MLA_HEADS, NOPE_DIM + V_DIM)
    wk_pad = jnp.pad(wkv[..., :NOPE_DIM], ((0, 0), (0, 0), (0, HEAD_PAD - NOPE_DIM)))
    wk_pad = wk_pad.reshape(KV_LORA, MLA_HEADS * HEAD_PAD)
    place = np.zeros((LANES, MLA_HEADS, HEAD_PAD), np.float32)
    for j in range(ROPE_DIM):
        place[j, :, NOPE_DIM + j] = 1.0
    wk2 = jnp.concatenate([wk_pad, jnp.asarray(place.reshape(LANES, -1))], axis=0).astype(BF16)
    wvt = wkv[..., NOPE_DIM:].reshape(KV_LORA, MLA_WIDTH).T.astype(BF16)

    w1_np, m3_np, wc_np = _dft_constants()
    w1 = jnp.asarray(w1_np, F32).astype(BF16)
    m3 = jnp.asarray(m3_np, F32).astype(BF16)
    wc = jnp.asarray(wc_np, F32).astype(BF16)

    cos_np, sin_np = _rope_tables(s)
    scale = QK_SCALE
    cosq = jnp.asarray((cos_np * scale).T, F32)
    sinq = jnp.asarray((sin_np * scale).T, F32)
    kpad = np.zeros((s, LANES - ROPE_DIM))
    cosk = jnp.asarray(np.concatenate([cos_np, kpad], axis=1), F32)
    sink = jnp.asarray(np.concatenate([sin_np, kpad], axis=1), F32)

    qg = q_norm_g[l][None, :]
    kvg = kv_norm_g[l][None, :]

    rows = ((b + 1 + 7) // 8) * 8
    cvec = jnp.concatenate([c, c_ctx[None, :], jnp.zeros((rows - b - 1, D_MODEL), c.dtype)], axis=0)
    mod = _modulation(cvec, w_ada[l].astype(BF16), b_ada[l][None, :]).reshape(rows, 3, D_MODEL)

    kc, vct = _ctx_proj(ctx, mod, b, w_ctx, b_ctx, kvg, wk2, wvt)
    qt, kl, vlt, gates, d = _lat_proj(x, mod, w_in_p, b_in_p, qg, wqt, kvg, wk2, wvt, wc,
                                      cosq, sinq, cosk, sink, tm=512)
    z = _fft(d, w1, m3)
    attn = _attention(qt, kc, kl, vct, vlt, tq=512, ck=2048, sck=256)
    return _merge(x, mod, attn, z, gates, w_fourier[l].astype(BF16), b_fourier[l][None, :],
                  w_out[l].astype(BF16), b_out[l][None, :], post_ln_g[l][None, :], post_ln_b[l][None, :], tm=512)
```

```python
import functools
import math

import jax
import jax.numpy as jnp
import numpy as np
from jax import lax
from jax.experimental import pallas as pl
from jax.experimental.pallas import tpu as pltpu

D_MODEL = 1024
CTX_LEN = 256
GRID_W = 64
MLA_HEADS = 8
NOPE_DIM = 64
ROPE_DIM = 32
V_DIM = 64
QK_DIM = NOPE_DIM + ROPE_DIM
Q_LORA = 256
KV_LORA = 128
MLA_WIDTH = MLA_HEADS * V_DIM
F_GROUPS = 4
F_GROUP_DIM = 128
F_WIDTH = F_GROUPS * F_GROUP_DIM
D_MIX = MLA_WIDTH + F_WIDTH
ROPE_BASE = 10000.0
LN_EPS = 1e-6
DEPTH = 1
DEEPNORM_ALPHA = (2.0 * DEPTH) ** 0.25
QK_SCALE = math.log2(math.e) / math.sqrt(QK_DIM)

LANES = 128
HEAD_PAD = 128
V_PAD = 80
FFT_RADIX = 64
X_PITCH = 72
Y_PITCH = 136
FFT_UNROLL = 8

COL_QLAT = 0
COL_CKV = 256
COL_KROPE = 384
COL_GMLA = 512
COL_FIN = 1024
COL_GF = 1536
D_IN_PAD = 2048

BF16 = jnp.bfloat16
F32 = jnp.float32

VMEM_LIMIT = 56 * 1024 * 1024


def _dot(a, b):
    return jnp.dot(a, b, preferred_element_type=F32)


def _dot_nt(a, b):
    return lax.dot_general(a, b, (((1,), (1,)), ((), ())), preferred_element_type=F32)


def _layer_norm_rows(x):
    mu = jnp.mean(x, axis=-1, keepdims=True)
    xc = x - mu
    var = jnp.mean(xc * xc, axis=-1, keepdims=True)
    return xc * lax.rsqrt(var + LN_EPS)


def _rms_rows(x, g):
    return x * lax.rsqrt(jnp.mean(x * x, axis=-1, keepdims=True) + LN_EPS) * g


def _silu(x):
    return x * jax.nn.sigmoid(x)


def _mod_kernel(c_ref, w_ref, b_ref, o_ref):
    a = _silu(c_ref[...]).astype(BF16)
    o_ref[...] = _dot(a, w_ref[...]) + b_ref[...]


def _modulation(cvec, w_ada, b_ada):
    rows = cvec.shape[0]
    n = w_ada.shape[1]
    tn = 1024
    return pl.pallas_call(
        _mod_kernel,
        grid=(n // tn,),
        in_specs=[pl.BlockSpec((rows, D_MODEL), lambda j: (0, 0)),
                  pl.BlockSpec((D_MODEL, tn), lambda j: (0, j)),
                  pl.BlockSpec((1, tn), lambda j: (0, j))],
        out_specs=pl.BlockSpec((rows, tn), lambda j: (0, j)),
        out_shape=jax.ShapeDtypeStruct((rows, n), F32),
        name="modulation",
    )(cvec, w_ada, b_ada)


def _store_heads(k_ref, vt_ref, k_all, vt_all):
    for hd in range(MLA_HEADS):
        k_ref[0, hd] = k_all[:, hd * HEAD_PAD:(hd + 1) * HEAD_PAD].astype(BF16)
        vt_ref[0, hd, :V_DIM, :] = vt_all[hd * V_DIM:(hd + 1) * V_DIM].astype(BF16)
        vt_ref[0, hd, V_DIM:, :] = jnp.ones((V_PAD - V_DIM, vt_all.shape[1]), BF16)


def _ctx_kernel(x_ref, mod_ref, w_ref, b_ref, kvg_ref, wk2_ref, wvt_ref, k_ref, vt_ref):
    y = _layer_norm_rows(x_ref[0])
    h = (y * (1.0 + mod_ref[0, 1:2, :]) + mod_ref[0, 0:1, :]).astype(BF16)
    proj = _dot(h, w_ref[...]) + b_ref[...]
    ckv_n = _rms_rows(proj[:, :KV_LORA], kvg_ref[...])
    ckv_b = ckv_n.astype(BF16)
    kin = jnp.concatenate([ckv_b, proj[:, KV_LORA:].astype(BF16)], axis=1)
    _store_heads(k_ref, vt_ref, _dot(kin, wk2_ref[...]), _dot_nt(wvt_ref[...], ckv_b))


def _ctx_proj(ctx, mod, mod_row, w_ctx, b_ctx, kvg, wk2, wvt):
    b, t, _ = ctx.shape
    const = lambda *shape: pl.BlockSpec(shape, lambda i: (0,) * len(shape))
    return pl.pallas_call(
        _ctx_kernel,
        grid=(b,),
        in_specs=[pl.BlockSpec((1, t, D_MODEL), lambda i: (i, 0, 0)),
                  pl.BlockSpec((1, 3, D_MODEL), lambda i: (mod_row, 0, 0)),
                  const(D_MODEL, 2 * LANES), const(1, 2 * LANES), const(1, KV_LORA),
                  const(2 * LANES, MLA_HEADS * HEAD_PAD), const(MLA_WIDTH, KV_LORA)],
        out_specs=[pl.BlockSpec((1, MLA_HEADS, t, HEAD_PAD), lambda i: (i, 0, 0, 0)),
                   pl.BlockSpec((1, MLA_HEADS, V_PAD, t), lambda i: (i, 0, 0, 0))],
        out_shape=[jax.ShapeDtypeStruct((b, MLA_HEADS, t, HEAD_PAD), BF16),
                   jax.ShapeDtypeStruct((b, MLA_HEADS, V_PAD, t), BF16)],
        compiler_params=pltpu.CompilerParams(vmem_limit_bytes=VMEM_LIMIT),
        name="ctx_proj",
    )(ctx, mod, w_ctx, b_ctx, kvg, wk2, wvt)


def _lat_kernel(x_ref, mod_ref, w_in_ref, b_in_ref, qg_ref, wqt_ref, kvg_ref, wk2_ref, wvt_ref, wc_ref,
                cosq_ref, sinq_ref, cosk_ref, sink_ref,
                qt_ref, k_ref, vt_ref, gates_ref, d_ref, h_scr):
    y = _layer_norm_rows(x_ref[0])
    h_scr[...] = (y * (1.0 + mod_ref[0, 1:2, :]) + mod_ref[0, 0:1, :]).astype(BF16)

    def proj(lo, hi):
        return _dot(h_scr[...], w_in_ref[:, lo:hi]) + b_in_ref[:, lo:hi]

    low = proj(COL_QLAT, COL_GMLA)

    qn = _rms_rows(low[:, COL_QLAT:COL_CKV], qg_ref[...]).astype(BF16)
    qt2 = _dot_nt(wqt_ref[...], qn)
    scale = QK_SCALE
    cosq = cosq_ref[...]
    sinq = sinq_ref[...]
    rot_base = MLA_HEADS * HEAD_PAD
    for hd in range(MLA_HEADS):
        base = hd * HEAD_PAD
        qt_ref[0, hd, :NOPE_DIM, :] = (qt2[base:base + NOPE_DIM] * scale).astype(BF16)
        rope = (qt2[base + NOPE_DIM:base + QK_DIM] * cosq
                + qt2[rot_base + hd * ROPE_DIM:rot_base + (hd + 1) * ROPE_DIM] * sinq)
        qt_ref[0, hd, NOPE_DIM:QK_DIM, :] = rope.astype(BF16)
        qt_ref[0, hd, QK_DIM:, :] = jnp.zeros((HEAD_PAD - QK_DIM, qt2.shape[1]), BF16)

    ckv_b = _rms_rows(low[:, COL_CKV:COL_KROPE], kvg_ref[...]).astype(BF16)
    kr = low[:, COL_KROPE:COL_GMLA]
    half = ROPE_DIM // 2
    rot = pltpu.roll(kr, half, axis=1) - pltpu.roll(kr, LANES - half, axis=1)
    kro = kr * cosk_ref[...] + rot * sink_ref[...]
    kin = jnp.concatenate([ckv_b, kro.astype(BF16)], axis=1)
    _store_heads(k_ref, vt_ref, _dot(kin, wk2_ref[...]), _dot_nt(wvt_ref[...], ckv_b))

    gates_ref[0, :, :MLA_WIDTH] = _silu(proj(COL_GMLA, COL_FIN)).astype(BF16)
    gates_ref[0, :, MLA_WIDTH:] = _silu(proj(COL_GF, D_IN_PAD)).astype(BF16)

    d_ref[0] = _dot(proj(COL_FIN, COL_GF).astype(BF16), wc_ref[...]).astype(BF16)


def _lat_proj(x, mod, w_in_p, b_in_p, qg, wqt, kvg, wk2, wvt, wc, cosq, sinq, cosk, sink, tm):
    b, s, _ = x.shape
    const = lambda *shape: pl.BlockSpec(shape, lambda i, j: (0,) * len(shape))
    hp = MLA_HEADS * HEAD_PAD
    return pl.pallas_call(
        _lat_kernel,
        grid=(b, s // tm),
        in_specs=[pl.BlockSpec((1, tm, D_MODEL), lambda i, j: (i, j, 0)),
                  pl.BlockSpec((1, 3, D_MODEL), lambda i, j: (i, 0, 0)),
                  const(D_MODEL, D_IN_PAD), const(1, D_IN_PAD), const(1, Q_LORA),
                  const(hp + MLA_HEADS * ROPE_DIM, Q_LORA), const(1, KV_LORA),
                  const(2 * LANES, hp), const(MLA_WIDTH, KV_LORA), const(F_WIDTH, 2 * F_WIDTH),
                  pl.BlockSpec((ROPE_DIM, tm), lambda i, j: (0, j)),
                  pl.BlockSpec((ROPE_DIM, tm), lambda i, j: (0, j)),
                  pl.BlockSpec((tm, LANES), lambda i, j: (j, 0)),
                  pl.BlockSpec((tm, LANES), lambda i, j: (j, 0))],
        out_specs=[pl.BlockSpec((1, MLA_HEADS, HEAD_PAD, tm), lambda i, j: (i, 0, 0, j)),
                   pl.BlockSpec((1, MLA_HEADS, tm, HEAD_PAD), lambda i, j: (i, 0, j, 0)),
                   pl.BlockSpec((1, MLA_HEADS, V_PAD, tm), lambda i, j: (i, 0, 0, j)),
                   pl.BlockSpec((1, tm, D_MIX), lambda i, j: (i, j, 0)),
                   pl.BlockSpec((1, tm, 2 * F_WIDTH), lambda i, j: (i, j, 0))],
        out_shape=[jax.ShapeDtypeStruct((b, MLA_HEADS, HEAD_PAD, s), BF16),
                   jax.ShapeDtypeStruct((b, MLA_HEADS, s, HEAD_PAD), BF16),
                   jax.ShapeDtypeStruct((b, MLA_HEADS, V_PAD, s), BF16),
                   jax.ShapeDtypeStruct((b, s, D_MIX), BF16),
                   jax.ShapeDtypeStruct((b, s, 2 * F_WIDTH), BF16)],
        scratch_shapes=[pltpu.VMEM((tm, D_MODEL), BF16)],
        compiler_params=pltpu.CompilerParams(vmem_limit_bytes=VMEM_LIMIT),
        name="lat_proj",
    )(x, mod, w_in_p, b_in_p, qg, wqt, kvg, wk2, wvt, wc, cosq, sinq, cosk, sink)


def _fft_kernel(dr_ref, di_ref, w1_ref, m3_ref, z_ref, xs, ys, zs):
    r = FFT_RADIX

    def fill(t1, carry):
        src = pl.multiple_of(t1 * r, r)
        dst = pl.multiple_of(t1 * X_PITCH, 8)
        re = dr_ref[0, pl.ds(src, r), :].astype(F32)
        im = di_ref[0, pl.ds(src, r), :].astype(F32)
        xs[0, pl.ds(dst, r), :] = re[:, :LANES]
        xs[1, pl.ds(dst, r), :] = re[:, LANES:]
        xs[2, pl.ds(dst, r), :] = im[:, :LANES]
        xs[3, pl.ds(dst, r), :] = im[:, LANES:]
        return carry

    lax.fori_loop(0, r, fill, 0)

    def stage1(g, carry):
        cols = []
        for u in range(FFT_UNROLL):
            t2 = g * FFT_UNROLL + u
            parts = [xs[sl, pl.ds(t2, r, stride=X_PITCH), :] for sl in range(4)]
            cols.append(jnp.concatenate([jnp.concatenate(parts[0:2], axis=1),
                                         jnp.concatenate(parts[2:4], axis=1)], axis=0))
        rhs = jnp.concatenate(cols, axis=1).astype(BF16)
        y = _dot(w1_ref[...], rhs)
        for u in range(FFT_UNROLL):
            dst = pl.multiple_of((g * FFT_UNROLL + u) * Y_PITCH, 8)
            ys[0, pl.ds(dst, 2 * r), :] = y[:, (2 * u) * LANES:(2 * u + 1) * LANES]
            ys[1, pl.ds(dst, 2 * r), :] = y[:, (2 * u + 1) * LANES:(2 * u + 2) * LANES]
        return carry

    lax.fori_loop(0, r // FFT_UNROLL, stage1, 0)

    def stage2(g, carry):
        for u in range(FFT_UNROLL):
            k1 = g * FFT_UNROLL + u
            yr = jnp.concatenate([ys[sl, pl.ds(k1, r, stride=Y_PITCH), :] for sl in range(2)], axis=1)
            yi = jnp.concatenate([ys[sl, pl.ds(r + k1, r, stride=Y_PITCH), :] for sl in range(2)], axis=1)
            rhs = jnp.concatenate([yr, yi], axis=0).astype(BF16)
            z = _dot(m3_ref[k1], rhs)
            zs[0, pl.ds(k1, r, stride=X_PITCH), :] = z[:, :LANES]
            zs[1, pl.ds(k1, r, stride=X_PITCH), :] = z[:, LANES:]
        return carry

    lax.fori_loop(0, r // FFT_UNROLL, stage2, 0)

    def drain(k2, carry):
        src = pl.multiple_of(k2 * X_PITCH, 8)
        dst = pl.multiple_of(k2 * r, r)
        z_ref[0, pl.ds(dst, r), :] = jnp.concatenate(
            [zs[0, pl.ds(src, r), :], zs[1, pl.ds(src, r), :]], axis=1).astype(BF16)
        return carry

    lax.fori_loop(0, r, drain, 0)


def _fft(d, w1, m3):
    b, s, _ = d.shape
    r = FFT_RADIX
    cw = 2 * LANES
    nblk = F_WIDTH // cw
    return pl.pallas_call(
        _fft_kernel,
        grid=(b, nblk),
        in_specs=[pl.BlockSpec((1, s, cw), lambda i, g: (i, 0, g)),
                  pl.BlockSpec((1, s, cw), lambda i, g: (i, 0, nblk + g)),
                  pl.BlockSpec((2 * r, 2 * r), lambda i, g: (0, 0)),
                  pl.BlockSpec((r, r, 2 * r), lambda i, g: (0, 0, 0))],
        out_specs=pl.BlockSpec((1, s, cw), lambda i, g: (i, 0, g)),
        out_shape=jax.ShapeDtypeStruct((b, s, F_WIDTH), BF16),
        scratch_shapes=[pltpu.VMEM((4, r * X_PITCH, LANES), F32),
                        pltpu.VMEM((2, r * Y_PITCH, LANES), F32),
                        pltpu.VMEM((2, r * X_PITCH, LANES), F32)],
        compiler_params=pltpu.CompilerParams(vmem_limit_bytes=VMEM_LIMIT),
        name="fft",
    )(d, d, w1, m3)


def _attn_kernel(qt_ref, kc_ref, kl_ref, vct_ref, vlt_ref, o_ref,
                 s_scr0, s_scr1, p_scr0, p_scr1, acc_scr, m_scr, o_scr, *, nq, sck):
    j = pl.program_id(1)
    tq = qt_ref.shape[3]
    n_ctx = kc_ref.shape[2]
    n_lat = kl_ref.shape[2]
    nh = MLA_HEADS
    n_body = 2
    lat_rows = n_lat // n_body
    ctx_rows = n_ctx // n_body

    def fold8(v, op):
        return op(v.reshape(v.shape[0] // 8, 8, tq), axis=0)

    s_scr = (s_scr0, s_scr1)
    p_scr = (p_scr0, p_scr1)

    def phase(h, s_qk, blk, do_qk, do_ex, do_pv):
        h_pv = (h + nh - 2) % nh
        s_ex = 1 - s_qk
        if do_ex:
            mb = jnp.broadcast_to(jnp.max(m_scr[s_ex], axis=0, keepdims=True), (8, tq))
        if do_pv:
            acc_scr[...] = jnp.zeros_like(acc_scr)

        def work(k_ref, src, row0, n, m8c):
            rows = pl.ds(row0, n)
            if do_ex:
                sv = s_scr[s_ex][rows, :]
                p = jnp.exp2(sv.reshape(n // 8, 8, tq) - mb[None]).reshape(n, tq)
                p_scr[s_ex][rows, :] = p.astype(BF16)
            if do_qk:
                s = _dot(k_ref[0, h, pl.ds(src, n), :], qt_ref[0, h])
                s_scr[s_qk][rows, :] = s
                m8c = jnp.maximum(m8c, fold8(s, jnp.max))
            return m8c

        def body(c, m8c):
            lat0 = pl.multiple_of(c * lat_rows, lat_rows)
            ctx0 = pl.multiple_of(c * ctx_rows, ctx_rows)
            if do_pv:
                acc_scr[...] += (
                    _dot(vlt_ref[0, h_pv, :, pl.ds(lat0, lat_rows)], p_scr[s_qk][pl.ds(lat0, lat_rows), :])
                    + _dot(vct_ref[0, h_pv, :, pl.ds(ctx0, ctx_rows)], p_scr[s_qk][pl.ds(n_lat + ctx0, ctx_rows), :]))
            for sub in range(lat_rows // sck):
                m8c = work(kl_ref, lat0 + sub * sck, lat0 + sub * sck, sck, m8c)
            return work(kc_ref, ctx0, n_lat + ctx0, ctx_rows, m8c)

        m8 = lax.fori_loop(0, n_body, body, jnp.full((8, tq), -jnp.inf, F32))
        if do_pv:
            pv_blk = jnp.where(jnp.asarray(h) < 2, blk - 1, blk)
            acc = acc_scr[...]
            o_scr[pv_blk % 2, h_pv] = acc[:V_DIM] / acc[V_DIM:V_DIM + 1]
        if do_qk:
            m_scr[s_qk] = m8

    @pl.when(j == 0)
    def _():
        phase(0, 0, 0, True, False, False)
        phase(1, 1, 0, True, True, False)

    @pl.when(j < nq)
    def _():
        def run(pair, carry):
            phase(2 * pair, 0, j, True, True, True)
            phase(2 * pair + 1, 1, j, True, True, True)
            return carry
        lax.fori_loop(jnp.where(j == 0, 1, 0), nh // 2, run, 0)

    @pl.when(j == nq)
    def _():
        phase(0, 0, nq, False, True, True)
        phase(1, 1, nq, False, False, True)

    @pl.when(j > 0)
    def _():
        done = o_scr[(j - 1) % 2]
        o_ref[0] = done.reshape(nh * V_DIM, tq).T.astype(BF16)


def _attention(qt, kc, kl, vct, vlt, tq, sck):
    b, nh, hp, s = qt.shape
    n_ctx = kc.shape[2]
    nq = s // tq
    once = dict(pipeline_mode=pl.Buffered(1))
    return pl.pallas_call(
        functools.partial(_attn_kernel, nq=nq, sck=sck),
        grid=(b, nq + 1),
        in_specs=[pl.BlockSpec((1, nh, hp, tq), lambda i, j: (i, 0, 0, jnp.minimum(j, nq - 1))),
                  pl.BlockSpec((1, nh, n_ctx, hp), lambda i, j: (i, 0, 0, 0), **once),
                  pl.BlockSpec((1, nh, s, hp), lambda i, j: (i, 0, 0, 0), **once),
                  pl.BlockSpec((1, nh, V_PAD, n_ctx), lambda i, j: (i, 0, 0, 0), **once),
                  pl.BlockSpec((1, nh, V_PAD, s), lambda i, j: (i, 0, 0, 0), **once)],
        out_specs=pl.BlockSpec((1, tq, MLA_WIDTH), lambda i, j: (i, jnp.maximum(j - 1, 0), 0)),
        out_shape=jax.ShapeDtypeStruct((b, s, MLA_WIDTH), BF16),
        scratch_shapes=[pltpu.VMEM((n_ctx + s, tq), F32), pltpu.VMEM((n_ctx + s, tq), F32),
                        pltpu.VMEM((n_ctx + s, tq), BF16), pltpu.VMEM((n_ctx + s, tq), BF16),
                        pltpu.VMEM((V_PAD, tq), F32),
                        pltpu.VMEM((2, 8, tq), F32),
                        pltpu.VMEM((2, nh, V_DIM, tq), F32)],
        compiler_params=pltpu.CompilerParams(vmem_limit_bytes=VMEM_LIMIT,
                                             dimension_semantics=("arbitrary", "arbitrary")),
        name="attention",
    )(qt, kc, kl, vct, vlt)


def _merge_kernel(x_ref, mod_ref, attn_ref, z_ref, gates_ref, wf_ref, bf_ref, wo_ref, bo_ref, g_ref, b_ref, o_ref):
    four = _dot(z_ref[0], wf_ref[...]) + bf_ref[...]
    gates = gates_ref[0]
    ya = attn_ref[0].astype(F32) * gates[:, :MLA_WIDTH].astype(F32)
    yf = four * gates[:, MLA_WIDTH:].astype(F32)
    ycat = jnp.concatenate([ya.astype(BF16), yf.astype(BF16)], axis=1)
    y = _dot(ycat, wo_ref[...]) + bo_ref[...]
    r = DEEPNORM_ALPHA * x_ref[0] + mod_ref[0, 2:3, :] * y
    o_ref[0] = _layer_norm_rows(r) * g_ref[...] + b_ref[...]


def _merge(x, mod, attn, z, gates, wf, bf, wo, bo, g, bb, tm):
    b, s, _ = x.shape
    const = lambda *shape: pl.BlockSpec(shape, lambda i, j: (0,) * len(shape))
    tok = lambda w: pl.BlockSpec((1, tm, w), lambda i, j: (i, j, 0))
    return pl.pallas_call(
        _merge_kernel,
        grid=(b, s // tm),
        in_specs=[tok(D_MODEL), pl.BlockSpec((1, 3, D_MODEL), lambda i, j: (i, 0, 0)),
                  tok(MLA_WIDTH), tok(F_WIDTH), tok(D_MIX),
                  const(F_WIDTH, F_WIDTH), const(1, F_WIDTH), const(D_MIX, D_MODEL), const(1, D_MODEL),
                  const(1, D_MODEL), const(1, D_MODEL)],
        out_specs=tok(D_MODEL),
        out_shape=jax.ShapeDtypeStruct((b, s, D_MODEL), x.dtype),
        compiler_params=pltpu.CompilerParams(vmem_limit_bytes=VMEM_LIMIT),
        name="merge",
    )(x, mod, attn, z, gates, wf, bf, wo, bo, g, bb)


def _rot_cols(w):
    half = w.shape[-1] // 2
    return jnp.concatenate([-w[..., half:], w[..., :half]], axis=-1)


def _rope_tables(s):
    n_rows = s // GRID_W
    pos = np.arange(s)
    rows = (pos // GRID_W).astype(np.float64)
    cols = (pos % GRID_W).astype(np.float64)
    assert n_rows * GRID_W == s
    axis_dim = ROPE_DIM // 2
    inv_freq = ROPE_BASE ** (-np.arange(0, axis_dim, 2, dtype=np.float64) / axis_dim)
    ang = np.concatenate([rows[:, None] * inv_freq, cols[:, None] * inv_freq], axis=-1)
    ang = np.concatenate([ang, ang], axis=-1)
    return np.cos(ang), np.sin(ang)


def _dft_constants():
    r = FFT_RADIX
    n = r * r
    k = np.arange(r, dtype=np.float64)
    ang1 = 2.0 * np.pi * np.outer(k, k) / r
    c1, s1 = np.cos(ang1), np.sin(ang1)
    w1 = np.block([[c1, s1], [-s1, c1]])
    k1 = np.arange(r)[:, None, None]
    k2 = np.arange(r)[None, :, None]
    t2 = np.arange(r)[None, None, :]
    ang3 = 2.0 * np.pi * t2 * (k1 + r * k2) / n
    m3 = np.concatenate([np.cos(ang3), np.sin(ang3)], axis=-1) / r
    c = np.arange(F_GROUP_DIM, dtype=np.float64)
    angc = 2.0 * np.pi * np.outer(c, c) / F_GROUP_DIM
    norm = 1.0 / math.sqrt(F_GROUP_DIM)
    eye = np.eye(F_GROUPS)
    wc = np.concatenate([np.kron(eye, np.cos(angc) * norm), -np.kron(eye, np.sin(angc) * norm)], axis=1)
    return w1, m3, wc


def kernel(x, c, ctx, c_ctx, w_ada, b_ada, w_in, b_in, q_norm_g, w_q_up, kv_norm_g, w_kv_up, w_fourier,
           b_fourier, w_out, b_out, post_ln_g, post_ln_b):
    b, s, _ = x.shape
    assert s == FFT_RADIX * FFT_RADIX and w_ada.shape[0] == 1
    l = 0

    wi, bi = w_in[l], b_in[l]
    o_q, o_kv, o_kr, o_gm, o_f, o_gf = 0, Q_LORA, Q_LORA + KV_LORA, Q_LORA + KV_LORA + ROPE_DIM, \
        Q_LORA + KV_LORA + ROPE_DIM + MLA_WIDTH, Q_LORA + KV_LORA + ROPE_DIM + MLA_WIDTH + F_WIDTH

    def in_cols(a):
        kr = a[..., o_kr:o_gm]
        zpad = jnp.zeros(a.shape[:-1] + (LANES - ROPE_DIM,), a.dtype)
        return jnp.concatenate([a[..., o_q:o_kr], kr, zpad, a[..., o_gm:]], axis=-1)

    w_in_p = in_cols(wi).astype(BF16)
    b_in_p = in_cols(bi)[None, :]
    w_ctx = w_in_p[:, COL_CKV:COL_GMLA]
    b_ctx = b_in_p[:, COL_CKV:COL_GMLA]

    wq = w_q_up[l].reshape(Q_LORA, MLA_HEADS, QK_DIM)
    wq_pad = jnp.pad(wq, ((0, 0), (0, 0), (0, HEAD_PAD - QK_DIM))).reshape(Q_LORA, MLA_HEADS * HEAD_PAD)
    wq_rot = _rot_cols(wq[..., NOPE_DIM:]).reshape(Q_LORA, MLA_HEADS * ROPE_DIM)
    wqt = jnp.concatenate([wq_pad, wq_rot], axis=1).T.astype(BF16)

    wkv = w_kv_up[l].reshape(KV_LORA, MLA_HEADS, NOPE_DIM + V_DIM)
    wk_pad = jnp.pad(wkv[..., :NOPE_DIM], ((0, 0), (0, 0), (0, HEAD_PAD - NOPE_DIM)))
    wk_pad = wk_pad.reshape(KV_LORA, MLA_HEADS * HEAD_PAD)
    place = np.zeros((LANES, MLA_HEADS, HEAD_PAD), np.float32)
    for j in range(ROPE_DIM):
        place[j, :, NOPE_DIM + j] = 1.0
    wk2 = jnp.concatenate([wk_pad, jnp.asarray(place.reshape(LANES, -1))], axis=0).astype(BF16)
    wvt = wkv[..., NOPE_DIM:].reshape(KV_LORA, MLA_WIDTH).T.astype(BF16)

    w1_np, m3_np, wc_np = _dft_constants()
    w1 = jnp.asarray(w1_np, F32).astype(BF16)
    m3 = jnp.asarray(m3_np, F32).astype(BF16)
    wc = jnp.asarray(wc_np, F32).astype(BF16)

    cos_np, sin_np = _rope_tables(s)
    scale = QK_SCALE
    cosq = jnp.asarray((cos_np * scale).T, F32)
    sinq = jnp.asarray((sin_np * scale).T, F32)
    kpad = np.zeros((s, LANES - ROPE_DIM))
    cosk = jnp.asarray(np.concatenate([cos_np, kpad], axis=1), F32)
    sink = jnp.asarray(np.concatenate([sin_np, kpad], axis=1), F32)

    qg = q_norm_g[l][None, :]
    kvg = kv_norm_g[l][None, :]

    rows = ((b + 1 + 7) // 8) * 8
    cvec = jnp.concatenate([c, c_ctx[None, :], jnp.zeros((rows - b - 1, D_MODEL), c.dtype)], axis=0)
    mod = _modulation(cvec, w_ada[l].astype(BF16), b_ada[l][None, :]).reshape(rows, 3, D_MODEL)

    kc, vct = _ctx_proj(ctx, mod, b, w_ctx, b_ctx, kvg, wk2, wvt)
    qt, kl, vlt, gates, d = _lat_proj(x, mod, w_in_p, b_in_p, qg, wqt, kvg, wk2, wvt, wc,
                                      cosq, sinq, cosk, sink, tm=512)
    z = _fft(d, w1, m3)
    attn = _attention(qt, kc, kl, vct, vlt, tq=512, sck=256)
    return _merge(x, mod, attn, z, gates, w_fourier[l].astype(BF16), b_fourier[l][None, :],
                  w_out[l].astype(BF16), b_out[l][None, :], post_ln_g[l][None, :], post_ln_b[l][None, :], tm=512)
```

```python
import functools
import math

import jax
import jax.numpy as jnp
import numpy as np
from jax import lax
from jax.experimental import pallas as pl
from jax.experimental.pallas import tpu as pltpu

D_MODEL = 1024
CTX_LEN = 256
GRID_W = 64
MLA_HEADS = 8
NOPE_DIM = 64
ROPE_DIM = 32
V_DIM = 64
QK_DIM = NOPE_DIM + ROPE_DIM
Q_LORA = 256
KV_LORA = 128
MLA_WIDTH = MLA_HEADS * V_DIM
F_GROUPS = 4
F_GROUP_DIM = 128
F_WIDTH = F_GROUPS * F_GROUP_DIM
D_MIX = MLA_WIDTH + F_WIDTH
ROPE_BASE = 10000.0
LN_EPS = 1e-6
DEPTH = 1
DEEPNORM_ALPHA = (2.0 * DEPTH) ** 0.25
QK_SCALE = math.log2(math.e) / math.sqrt(QK_DIM)

LANES = 128
HEAD_PAD = 128
V_PAD = 80
FFT_RADIX = 64
X_PITCH = 72
Y_PITCH = 136
FFT_UNROLL = 8

COL_QLAT = 0
COL_CKV = 256
COL_KROPE = 384
COL_GMLA = 512
COL_FIN = 1024
COL_GF = 1536
D_IN_PAD = 2048

BF16 = jnp.bfloat16
F32 = jnp.float32

VMEM_LIMIT = 56 * 1024 * 1024


def _dot(a, b):
    return jnp.dot(a, b, preferred_element_type=F32)


def _dot_nt(a, b):
    return lax.dot_general(a, b, (((1,), (1,)), ((), ())), preferred_element_type=F32)


def _layer_norm_rows(x):
    mu = jnp.mean(x, axis=-1, keepdims=True)
    xc = x - mu
    var = jnp.mean(xc * xc, axis=-1, keepdims=True)
    return xc * lax.rsqrt(var + LN_EPS)


def _rms_rows(x, g):
    return x * lax.rsqrt(jnp.mean(x * x, axis=-1, keepdims=True) + LN_EPS) * g


def _silu(x):
    return x * jax.nn.sigmoid(x)


def _mod_kernel(c_ref, w_ref, b_ref, o_ref):
    a = _silu(c_ref[...]).astype(BF16)
    o_ref[...] = _dot(a, w_ref[...]) + b_ref[...]


def _modulation(cvec, w_ada, b_ada):
    rows = cvec.shape[0]
    n = w_ada.shape[1]
    tn = 1024
    return pl.pallas_call(
        _mod_kernel,
        grid=(n // tn,),
        in_specs=[pl.BlockSpec((rows, D_MODEL), lambda j: (0, 0)),
                  pl.BlockSpec((D_MODEL, tn), lambda j: (0, j)),
                  pl.BlockSpec((1, tn), lambda j: (0, j))],
        out_specs=pl.BlockSpec((rows, tn), lambda j: (0, j)),
        out_shape=jax.ShapeDtypeStruct((rows, n), F32),
        name="modulation",
    )(cvec, w_ada, b_ada)


def _store_heads(k_ref, vt_ref, k_all, vt_all):
    for hd in range(MLA_HEADS):
        k_ref[0, hd] = k_all[:, hd * HEAD_PAD:(hd + 1) * HEAD_PAD].astype(BF16)
        vt_ref[0, hd, :V_DIM, :] = vt_all[hd * V_DIM:(hd + 1) * V_DIM].astype(BF16)
        vt_ref[0, hd, V_DIM:, :] = jnp.ones((V_PAD - V_DIM, vt_all.shape[1]), BF16)


def _ctx_kernel(x_ref, mod_ref, w_ref, b_ref, kvg_ref, wk2_ref, wvt_ref, k_ref, vt_ref):
    y = _layer_norm_rows(x_ref[0])
    h = (y * (1.0 + mod_ref[0, 1:2, :]) + mod_ref[0, 0:1, :]).astype(BF16)
    proj = _dot(h, w_ref[...]) + b_ref[...]
    ckv_n = _rms_rows(proj[:, :KV_LORA], kvg_ref[...])
    ckv_b = ckv_n.astype(BF16)
    kin = jnp.concatenate([ckv_b, proj[:, KV_LORA:].astype(BF16)], axis=1)
    _store_heads(k_ref, vt_ref, _dot(kin, wk2_ref[...]), _dot_nt(wvt_ref[...], ckv_b))


def _ctx_proj(ctx, mod, mod_row, w_ctx, b_ctx, kvg, wk2, wvt):
    b, t, _ = ctx.shape
    const = lambda *shape: pl.BlockSpec(shape, lambda i: (0,) * len(shape))
    return pl.pallas_call(
        _ctx_kernel,
        grid=(b,),
        in_specs=[pl.BlockSpec((1, t, D_MODEL), lambda i: (i, 0, 0)),
                  pl.BlockSpec((1, 3, D_MODEL), lambda i: (mod_row, 0, 0)),
                  const(D_MODEL, 2 * LANES), const(1, 2 * LANES), const(1, KV_LORA),
                  const(2 * LANES, MLA_HEADS * HEAD_PAD), const(MLA_WIDTH, KV_LORA)],
        out_specs=[pl.BlockSpec((1, MLA_HEADS, t, HEAD_PAD), lambda i: (i, 0, 0, 0)),
                   pl.BlockSpec((1, MLA_HEADS, V_PAD, t), lambda i: (i, 0, 0, 0))],
        out_shape=[jax.ShapeDtypeStruct((b, MLA_HEADS, t, HEAD_PAD), BF16),
                   jax.ShapeDtypeStruct((b, MLA_HEADS, V_PAD, t), BF16)],
        compiler_params=pltpu.CompilerParams(vmem_limit_bytes=VMEM_LIMIT),
        name="ctx_proj",
    )(ctx, mod, w_ctx, b_ctx, kvg, wk2, wvt)


def _lat_kernel(x_ref, mod_ref, w_in_ref, b_in_ref, qg_ref, wqt_ref, kvg_ref, wk2_ref, wvt_ref, wc_ref,
                cosq_ref, sinq_ref, cosk_ref, sink_ref,
                qt_ref, k_ref, vt_ref, gates_ref, d_ref, h_scr):
    y = _layer_norm_rows(x_ref[0])
    h_scr[...] = (y * (1.0 + mod_ref[0, 1:2, :]) + mod_ref[0, 0:1, :]).astype(BF16)

    def proj(lo, hi):
        return _dot(h_scr[...], w_in_ref[:, lo:hi]) + b_in_ref[:, lo:hi]

    low = proj(COL_QLAT, COL_GMLA)

    qn = _rms_rows(low[:, COL_QLAT:COL_CKV], qg_ref[...]).astype(BF16)
    qt2 = _dot_nt(wqt_ref[...], qn)
    scale = QK_SCALE
    cosq = cosq_ref[...]
    sinq = sinq_ref[...]
    rot_base = MLA_HEADS * HEAD_PAD
    for hd in range(MLA_HEADS):
        base = hd * HEAD_PAD
        qt_ref[0, hd, :NOPE_DIM, :] = (qt2[base:base + NOPE_DIM] * scale).astype(BF16)
        rope = (qt2[base + NOPE_DIM:base + QK_DIM] * cosq
                + qt2[rot_base + hd * ROPE_DIM:rot_base + (hd + 1) * ROPE_DIM] * sinq)
        qt_ref[0, hd, NOPE_DIM:QK_DIM, :] = rope.astype(BF16)
        qt_ref[0, hd, QK_DIM:, :] = jnp.zeros((HEAD_PAD - QK_DIM, qt2.shape[1]), BF16)

    ckv_b = _rms_rows(low[:, COL_CKV:COL_KROPE], kvg_ref[...]).astype(BF16)
    kr = low[:, COL_KROPE:COL_GMLA]
    half = ROPE_DIM // 2
    rot = pltpu.roll(kr, half, axis=1) - pltpu.roll(kr, LANES - half, axis=1)
    kro = kr * cosk_ref[...] + rot * sink_ref[...]
    kin = jnp.concatenate([ckv_b, kro.astype(BF16)], axis=1)
    _store_heads(k_ref, vt_ref, _dot(kin, wk2_ref[...]), _dot_nt(wvt_ref[...], ckv_b))

    gates_ref[0, :, :MLA_WIDTH] = _silu(proj(COL_GMLA, COL_FIN)).astype(BF16)
    gates_ref[0, :, MLA_WIDTH:] = _silu(proj(COL_GF, D_IN_PAD)).astype(BF16)

    d_ref[0] = _dot(proj(COL_FIN, COL_GF).astype(BF16), wc_ref[...]).astype(BF16)


def _lat_proj(x, mod, w_in_p, b_in_p, qg, wqt, kvg, wk2, wvt, wc, cosq, sinq, cosk, sink, tm):
    b, s, _ = x.shape
    const = lambda *shape: pl.BlockSpec(shape, lambda i, j: (0,) * len(shape))
    hp = MLA_HEADS * HEAD_PAD
    return pl.pallas_call(
        _lat_kernel,
        grid=(b, s // tm),
        in_specs=[pl.BlockSpec((1, tm, D_MODEL), lambda i, j: (i, j, 0)),
                  pl.BlockSpec((1, 3, D_MODEL), lambda i, j: (i, 0, 0)),
                  const(D_MODEL, D_IN_PAD), const(1, D_IN_PAD), const(1, Q_LORA),
                  const(hp + MLA_HEADS * ROPE_DIM, Q_LORA), const(1, KV_LORA),
                  const(2 * LANES, hp), const(MLA_WIDTH, KV_LORA), const(F_WIDTH, 2 * F_WIDTH),
                  pl.BlockSpec((ROPE_DIM, tm), lambda i, j: (0, j)),
                  pl.BlockSpec((ROPE_DIM, tm), lambda i, j: (0, j)),
                  pl.BlockSpec((tm, LANES), lambda i, j: (j, 0)),
                  pl.BlockSpec((tm, LANES), lambda i, j: (j, 0))],
        out_specs=[pl.BlockSpec((1, MLA_HEADS, HEAD_PAD, tm), lambda i, j: (i, 0, 0, j)),
                   pl.BlockSpec((1, MLA_HEADS, tm, HEAD_PAD), lambda i, j: (i, 0, j, 0)),
                   pl.BlockSpec((1, MLA_HEADS, V_PAD, tm), lambda i, j: (i, 0, 0, j)),
                   pl.BlockSpec((1, tm, D_MIX), lambda i, j: (i, j, 0)),
                   pl.BlockSpec((1, tm, 2 * F_WIDTH), lambda i, j: (i, j, 0))],
        out_shape=[jax.ShapeDtypeStruct((b, MLA_HEADS, HEAD_PAD, s), BF16),
                   jax.ShapeDtypeStruct((b, MLA_HEADS, s, HEAD_PAD), BF16),
                   jax.ShapeDtypeStruct((b, MLA_HEADS, V_PAD, s), BF16),
                   jax.ShapeDtypeStruct((b, s, D_MIX), BF16),
                   jax.ShapeDtypeStruct((b, s, 2 * F_WIDTH), BF16)],
        scratch_shapes=[pltpu.VMEM((tm, D_MODEL), BF16)],
        compiler_params=pltpu.CompilerParams(vmem_limit_bytes=VMEM_LIMIT),
        name="lat_proj",
    )(x, mod, w_in_p, b_in_p, qg, wqt, kvg, wk2, wvt, wc, cosq, sinq, cosk, sink)


def _fft_kernel(dr_ref, di_ref, w1_ref, m3_ref, z_ref, xs, ys, zs):
    r = FFT_RADIX

    def fill(t1, carry):
        src = pl.multiple_of(t1 * r, r)
        dst = pl.multiple_of(t1 * X_PITCH, 8)
        re = dr_ref[0, pl.ds(src, r), :].astype(F32)
        im = di_ref[0, pl.ds(src, r), :].astype(F32)
        xs[0, pl.ds(dst, r), :] = re[:, :LANES]
        xs[1, pl.ds(dst, r), :] = re[:, LANES:]
        xs[2, pl.ds(dst, r), :] = im[:, :LANES]
        xs[3, pl.ds(dst, r), :] = im[:, LANES:]
        return carry

    lax.fori_loop(0, r, fill, 0)

    def stage1(g, carry):
        cols = []
        for u in range(FFT_UNROLL):
            t2 = g * FFT_UNROLL + u
            parts = [xs[sl, pl.ds(t2, r, stride=X_PITCH), :] for sl in range(4)]
            cols.append(jnp.concatenate([jnp.concatenate(parts[0:2], axis=1),
                                         jnp.concatenate(parts[2:4], axis=1)], axis=0))
        rhs = jnp.concatenate(cols, axis=1).astype(BF16)
        y = _dot(w1_ref[...], rhs)
        for u in range(FFT_UNROLL):
            dst = pl.multiple_of((g * FFT_UNROLL + u) * Y_PITCH, 8)
            ys[0, pl.ds(dst, 2 * r), :] = y[:, (2 * u) * LANES:(2 * u + 1) * LANES]
            ys[1, pl.ds(dst, 2 * r), :] = y[:, (2 * u + 1) * LANES:(2 * u + 2) * LANES]
        return carry

    lax.fori_loop(0, r // FFT_UNROLL, stage1, 0)

    def stage2(g, carry):
        for u in range(FFT_UNROLL):
            k1 = g * FFT_UNROLL + u
            yr = jnp.concatenate([ys[sl, pl.ds(k1, r, stride=Y_PITCH), :] for sl in range(2)], axis=1)
            yi = jnp.concatenate([ys[sl, pl.ds(r + k1, r, stride=Y_PITCH), :] for sl in range(2)], axis=1)
            rhs = jnp.concatenate([yr, yi], axis=0).astype(BF16)
            z = _dot(m3_ref[k1], rhs)
            zs[0, pl.ds(k1, r, stride=X_PITCH), :] = z[:, :LANES]
            zs[1, pl.ds(k1, r, stride=X_PITCH), :] = z[:, LANES:]
        return carry

    lax.fori_loop(0, r // FFT_UNROLL, stage2, 0)

    def drain(k2, carry):
        src = pl.multiple_of(k2 * X_PITCH, 8)
        dst = pl.multiple_of(k2 * r, r)
        z_ref[0, pl.ds(dst, r), :] = jnp.concatenate(
            [zs[0, pl.ds(src, r), :], zs[1, pl.ds(src, r), :]], axis=1).astype(BF16)
        return carry

    lax.fori_loop(0, r, drain, 0)


def _fft(d, w1, m3):
    b, s, _ = d.shape
    r = FFT_RADIX
    cw = 2 * LANES
    nblk = F_WIDTH // cw
    return pl.pallas_call(
        _fft_kernel,
        grid=(b, nblk),
        in_specs=[pl.BlockSpec((1, s, cw), lambda i, g: (i, 0, g)),
                  pl.BlockSpec((1, s, cw), lambda i, g: (i, 0, nblk + g)),
                  pl.BlockSpec((2 * r, 2 * r), lambda i, g: (0, 0)),
                  pl.BlockSpec((r, r, 2 * r), lambda i, g: (0, 0, 0))],
        out_specs=pl.BlockSpec((1, s, cw), lambda i, g: (i, 0, g)),
        out_shape=jax.ShapeDtypeStruct((b, s, F_WIDTH), BF16),
        scratch_shapes=[pltpu.VMEM((4, r * X_PITCH, LANES), F32),
                        pltpu.VMEM((2, r * Y_PITCH, LANES), F32),
                        pltpu.VMEM((2, r * X_PITCH, LANES), F32)],
        compiler_params=pltpu.CompilerParams(vmem_limit_bytes=VMEM_LIMIT),
        name="fft",
    )(d, d, w1, m3)


def _attn_kernel(qt_ref, kc_ref, kl_ref, vct_ref, vlt_ref, o_ref,
                 s_scr0, s_scr1, p_scr0, p_scr1, acc_scr, m_scr, o_scr, *, nq, sck):
    j = pl.program_id(1)
    tq = qt_ref.shape[3]
    n_ctx = kc_ref.shape[2]
    n_lat = kl_ref.shape[2]
    nh = MLA_HEADS
    n_body = 2
    lat_rows = n_lat // n_body
    ctx_rows = n_ctx // n_body

    def fold8(v, op):
        return op(v.reshape(v.shape[0] // 8, 8, tq), axis=0)

    s_scr = (s_scr0, s_scr1)
    p_scr = (p_scr0, p_scr1)

    def phase(h, s_qk, blk, do_qk, do_ex, do_pv):
        h_pv = (h + nh - 2) % nh
        s_ex = 1 - s_qk
        if do_ex:
            mb = jnp.broadcast_to(jnp.max(m_scr[s_ex], axis=0, keepdims=True), (8, tq))
        if do_pv:
            acc_scr[...] = jnp.zeros_like(acc_scr)

        def work(k_ref, src, row0, n, m8c):
            rows = pl.ds(row0, n)
            if do_ex:
                sv = s_scr[s_ex][rows, :]
                p = jnp.exp2(sv.reshape(n // 8, 8, tq) - mb[None]).reshape(n, tq)
                p_scr[s_ex][rows, :] = p.astype(BF16)
            if do_qk:
                s = _dot(k_ref[0, h, pl.ds(src, n), :], qt_ref[0, h])
                s_scr[s_qk][rows, :] = s
                m8c = jnp.maximum(m8c, fold8(s, jnp.max))
            return m8c

        def body(c, m8c):
            lat0 = pl.multiple_of(c * lat_rows, lat_rows)
            ctx0 = pl.multiple_of(c * ctx_rows, ctx_rows)
            for sub in range(lat_rows // sck):
                m8c = work(kl_ref, lat0 + sub * sck, lat0 + sub * sck, sck, m8c)
            m8c = work(kc_ref, ctx0, n_lat + ctx0, ctx_rows, m8c)
            if do_pv:
                acc_scr[...] += (
                    _dot(vlt_ref[0, h_pv, :, pl.ds(lat0, lat_rows)], p_scr[s_qk][pl.ds(lat0, lat_rows), :])
                    + _dot(vct_ref[0, h_pv, :, pl.ds(ctx0, ctx_rows)], p_scr[s_qk][pl.ds(n_lat + ctx0, ctx_rows), :]))
            return m8c

        m8 = lax.fori_loop(0, n_body, body, jnp.full((8, tq), -jnp.inf, F32))
        if do_pv:
            pv_blk = jnp.where(jnp.asarray(h) < 2, blk - 1, blk)
            acc = acc_scr[...]
            o_scr[pv_blk % 2, h_pv] = acc[:V_DIM] / acc[V_DIM:V_DIM + 1]
        if do_qk:
            m_scr[s_qk] = m8

    @pl.when(j == 0)
    def _():
        phase(0, 0, 0, True, False, False)
        phase(1, 1, 0, True, True, False)

    @pl.when(j < nq)
    def _():
        def run(pair, carry):
            phase(2 * pair, 0, j, True, True, True)
            phase(2 * pair + 1, 1, j, True, True, True)
            return carry
        lax.fori_loop(jnp.where(j == 0, 1, 0), nh // 2, run, 0)

    @pl.when(j == nq)
    def _():
        phase(0, 0, nq, False, True, True)
        phase(1, 1, nq, False, False, True)

    @pl.when(j > 0)
    def _():
        done = o_scr[(j - 1) % 2]
        o_ref[0] = done.reshape(nh * V_DIM, tq).T.astype(BF16)


def _attention(qt, kc, kl, vct, vlt, tq, sck):
    b, nh, hp, s = qt.shape
    n_ctx = kc.shape[2]
    nq = s // tq
    once = dict(pipeline_mode=pl.Buffered(1))
    return pl.pallas_call(
        functools.partial(_attn_kernel, nq=nq, sck=sck),
        grid=(b, nq + 1),
        in_specs=[pl.BlockSpec((1, nh, hp, tq), lambda i, j: (i, 0, 0, jnp.minimum(j, nq - 1))),
                  pl.BlockSpec((1, nh, n_ctx, hp), lambda i, j: (i, 0, 0, 0), **once),
                  pl.BlockSpec((1, nh, s, hp), lambda i, j: (i, 0, 0, 0), **once),
                  pl.BlockSpec((1, nh, V_PAD, n_ctx), lambda i, j: (i, 0, 0, 0), **once),
                  pl.BlockSpec((1, nh, V_PAD, s), lambda i, j: (i, 0, 0, 0), **once)],
        out_specs=pl.BlockSpec((1, tq, MLA_WIDTH), lambda i, j: (i, jnp.maximum(j - 1, 0), 0)),
        out_shape=jax.ShapeDtypeStruct((b, s, MLA_WIDTH), BF16),
        scratch_shapes=[pltpu.VMEM((n_ctx + s, tq), F32), pltpu.VMEM((n_ctx + s, tq), F32),
                        pltpu.VMEM((n_ctx + s, tq), BF16), pltpu.VMEM((n_ctx + s, tq), BF16),
                        pltpu.VMEM((V_PAD, tq), F32),
                        pltpu.VMEM((2, 8, tq), F32),
                        pltpu.VMEM((2, nh, V_DIM, tq), F32)],
        compiler_params=pltpu.CompilerParams(vmem_limit_bytes=VMEM_LIMIT,
                                             dimension_semantics=("arbitrary", "arbitrary")),
        name="attention",
    )(qt, kc, kl, vct, vlt)


def _merge_kernel(x_ref, mod_ref, attn_ref, z_ref, gates_ref, wf_ref, bf_ref, wo_ref, bo_ref, g_ref, b_ref, o_ref):
    four = _dot(z_ref[0], wf_ref[...]) + bf_ref[...]
    gates = gates_ref[0]
    ya = attn_ref[0].astype(F32) * gates[:, :MLA_WIDTH].astype(F32)
    yf = four * gates[:, MLA_WIDTH:].astype(F32)
    ycat = jnp.concatenate([ya.astype(BF16), yf.astype(BF16)], axis=1)
    y = _dot(ycat, wo_ref[...]) + bo_ref[...]
    r = DEEPNORM_ALPHA * x_ref[0] + mod_ref[0, 2:3, :] * y
    o_ref[0] = _layer_norm_rows(r) * g_ref[...] + b_ref[...]


def _merge(x, mod, attn, z, gates, wf, bf, wo, bo, g, bb, tm):
    b, s, _ = x.shape
    const = lambda *shape: pl.BlockSpec(shape, lambda i, j: (0,) * len(shape))
    tok = lambda w: pl.BlockSpec((1, tm, w), lambda i, j: (i, j, 0))
    return pl.pallas_call(
        _merge_kernel,
        grid=(b, s // tm),
        in_specs=[tok(D_MODEL), pl.BlockSpec((1, 3, D_MODEL), lambda i, j: (i, 0, 0)),
                  tok(MLA_WIDTH), tok(F_WIDTH), tok(D_MIX),
                  const(F_WIDTH, F_WIDTH), const(1, F_WIDTH), const(D_MIX, D_MODEL), const(1, D_MODEL),
                  const(1, D_MODEL), const(1, D_MODEL)],
        out_specs=tok(D_MODEL),
        out_shape=jax.ShapeDtypeStruct((b, s, D_MODEL), x.dtype),
        compiler_params=pltpu.CompilerParams(vmem_limit_bytes=VMEM_LIMIT),
        name="merge",
    )(x, mod, attn, z, gates, wf, bf, wo, bo, g, bb)


def _rot_cols(w):
    half = w.shape[-1] // 2
    return jnp.concatenate([-w[..., half:], w[..., :half]], axis=-1)


def _rope_tables(s):
    n_rows = s // GRID_W
    pos = np.arange(s)
    rows = (pos // GRID_W).astype(np.float64)
    cols = (pos % GRID_W).astype(np.float64)
    assert n_rows * GRID_W == s
    axis_dim = ROPE_DIM // 2
    inv_freq = ROPE_BASE ** (-np.arange(0, axis_dim, 2, dtype=np.float64) / axis_dim)
    ang = np.concatenate([rows[:, None] * inv_freq, cols[:, None] * inv_freq], axis=-1)
    ang = np.concatenate([ang, ang], axis=-1)
    return np.cos(ang), np.sin(ang)


def _dft_constants():
    r = FFT_RADIX
    n = r * r
    k = np.arange(r, dtype=np.float64)
    ang1 = 2.0 * np.pi * np.outer(k, k) / r
    c1, s1 = np.cos(ang1), np.sin(ang1)
    w1 = np.block([[c1, s1], [-s1, c1]])
    k1 = np.arange(r)[:, None, None]
    k2 = np.arange(r)[None, :, None]
    t2 = np.arange(r)[None, None, :]
    ang3 = 2.0 * np.pi * t2 * (k1 + r * k2) / n
    m3 = np.concatenate([np.cos(ang3), np.sin(ang3)], axis=-1) / r
    c = np.arange(F_GROUP_DIM, dtype=np.float64)
    angc = 2.0 * np.pi * np.outer(c, c) / F_GROUP_DIM
    norm = 1.0 / math.sqrt(F_GROUP_DIM)
    eye = np.eye(F_GROUPS)
    wc = np.concatenate([np.kron(eye, np.cos(angc) * norm), -np.kron(eye, np.sin(angc) * norm)], axis=1)
    return w1, m3, wc


def kernel(x, c, ctx, c_ctx, w_ada, b_ada, w_in, b_in, q_norm_g, w_q_up, kv_norm_g, w_kv_up, w_fourier,
           b_fourier, w_out, b_out, post_ln_g, post_ln_b):
    b, s, _ = x.shape
    assert s == FFT_RADIX * FFT_RADIX and w_ada.shape[0] == 1
    l = 0

    wi, bi = w_in[l], b_in[l]
    o_q, o_kv, o_kr, o_gm, o_f, o_gf = 0, Q_LORA, Q_LORA + KV_LORA, Q_LORA + KV_LORA + ROPE_DIM, \
        Q_LORA + KV_LORA + ROPE_DIM + MLA_WIDTH, Q_LORA + KV_LORA + ROPE_DIM + MLA_WIDTH + F_WIDTH

    def in_cols(a):
        kr = a[..., o_kr:o_gm]
        zpad = jnp.zeros(a.shape[:-1] + (LANES - ROPE_DIM,), a.dtype)
        return jnp.concatenate([a[..., o_q:o_kr], kr, zpad, a[..., o_gm:]], axis=-1)

    w_in_p = in_cols(wi).astype(BF16)
    b_in_p = in_cols(bi)[None, :]
    w_ctx = w_in_p[:, COL_CKV:COL_GMLA]
    b_ctx = b_in_p[:, COL_CKV:COL_GMLA]

    wq = w_q_up[l].reshape(Q_LORA, MLA_HEADS, QK_DIM)
    wq_pad = jnp.pad(wq, ((0, 0), (0, 0), (0, HEAD_PAD - QK_DIM))).reshape(Q_LORA, MLA_HEADS * HEAD_PAD)
    wq_rot = _rot_cols(wq[..., NOPE_DIM:]).reshape(Q_LORA, MLA_HEADS * ROPE_DIM)
    wqt = jnp.concatenate([wq_pad, wq_rot], axis=1).T.astype(BF16)

    wkv = w_kv_up[l].reshape(KV_LORA, MLA_HEADS, NOPE_DIM + V_DIM)
    wk_pad = jnp.pad(wkv[..., :NOPE_DIM], ((0, 0), (0, 0), (0, HEAD_PAD - NOPE_DIM)))
    wk_pad = wk_pad.reshape(KV_LORA, MLA_HEADS * HEAD_PAD)
    place = np.zeros((LANES, MLA_HEADS, HEAD_PAD), np.float32)
    for j in range(ROPE_DIM):
        place[j, :, NOPE_DIM + j] = 1.0
    wk2 = jnp.concatenate([wk_pad, jnp.asarray(place.reshape(LANES, -1))], axis=0).astype(BF16)
    wvt = wkv[..., NOPE_DIM:].reshape(KV_LORA, MLA_WIDTH).T.astype(BF16)

    w1_np, m3_np, wc_np = _dft_constants()
    w1 = jnp.asarray(w1_np, F32).astype(BF16)
    m3 = jnp.asarray(m3_np, F32).astype(BF16)
    wc = jnp.asarray(wc_np, F32).astype(BF16)

    cos_np, sin_np = _rope_tables(s)
    scale = QK_SCALE
    cosq = jnp.asarray((cos_np * scale).T, F32)
    sinq = jnp.asarray((sin_np * scale).T, F32)
    kpad = np.zeros((s, LANES - ROPE_DIM))
    cosk = jnp.asarray(np.concatenate([cos_np, kpad], axis=1), F32)
    sink = jnp.asarray(np.concatenate([sin_np, kpad], axis=1), F32)

    qg = q_norm_g[l][None, :]
    kvg = kv_norm_g[l][None, :]

    rows = ((b + 1 + 7) // 8) * 8
    cvec = jnp.concatenate([c, c_ctx[None, :], jnp.zeros((rows - b - 1, D_MODEL), c.dtype)], axis=0)
    mod = _modulation(cvec, w_ada[l].astype(BF16), b_ada[l][None, :]).reshape(rows, 3, D_MODEL)

    kc, vct = _ctx_proj(ctx, mod, b, w_ctx, b_ctx, kvg, wk2, wvt)
    qt, kl, vlt, gates, d = _lat_proj(x, mod, w_in_p, b_in_p, qg, wqt, kvg, wk2, wvt, wc,
                                      cosq, sinq, cosk, sink, tm=512)
    z = _fft(d, w1, m3)
    attn = _attention(qt, kc, kl, vct, vlt, tq=512, sck=128)
    return _merge(x, mod, attn, z, gates, w_fourier[l].astype(BF16), b_fourier[l][None, :],
                  w_out[l].astype(BF16), b_out[l][None, :], post_ln_g[l][None, :], post_ln_b[l][None, :], tm=512)
```

```python
import functools
import math

import jax
import jax.numpy as jnp
import numpy as np
from jax import lax
from jax.experimental import pallas as pl
from jax.experimental.pallas import tpu as pltpu

D_MODEL = 1024
CTX_LEN = 256
GRID_W = 64
MLA_HEADS = 8
NOPE_DIM = 64
ROPE_DIM = 32
V_DIM = 64
QK_DIM = NOPE_DIM + ROPE_DIM
Q_LORA = 256
KV_LORA = 128
MLA_WIDTH = MLA_HEADS * V_DIM
F_GROUPS = 4
F_GROUP_DIM = 128
F_WIDTH = F_GROUPS * F_GROUP_DIM
D_MIX = MLA_WIDTH + F_WIDTH
ROPE_BASE = 10000.0
LN_EPS = 1e-6
DEPTH = 1
DEEPNORM_ALPHA = (2.0 * DEPTH) ** 0.25
QK_SCALE = math.log2(math.e) / math.sqrt(QK_DIM)

LANES = 128
HEAD_PAD = 128
V_PAD = 80
FFT_RADIX = 64
X_PITCH = 72
Y_PITCH = 136
FFT_UNROLL = 16

COL_QLAT = 0
COL_CKV = 256
COL_KROPE = 384
COL_GMLA = 512
COL_FIN = 1024
COL_GF = 1536
D_IN_PAD = 2048

BF16 = jnp.bfloat16
F32 = jnp.float32

VMEM_LIMIT = 56 * 1024 * 1024


def _dot(a, b):
    return jnp.dot(a, b, preferred_element_type=F32)


def _dot_nt(a, b):
    return lax.dot_general(a, b, (((1,), (1,)), ((), ())), preferred_element_type=F32)


def _layer_norm_rows(x):
    mu = jnp.mean(x, axis=-1, keepdims=True)
    xc = x - mu
    var = jnp.mean(xc * xc, axis=-1, keepdims=True)
    return xc * lax.rsqrt(var + LN_EPS)


def _rms_rows(x, g):
    return x * lax.rsqrt(jnp.mean(x * x, axis=-1, keepdims=True) + LN_EPS) * g


def _silu(x):
    return x * jax.nn.sigmoid(x)


def _mod_kernel(c_ref, w_ref, b_ref, o_ref):
    a = _silu(c_ref[...]).astype(BF16)
    o_ref[...] = _dot(a, w_ref[...].astype(BF16)) + b_ref[...]


def _modulation(cvec, w_ada, b_ada):
    rows = cvec.shape[0]
    n = w_ada.shape[1]
    tn = 1024
    return pl.pallas_call(
        _mod_kernel,
        grid=(n // tn,),
        in_specs=[pl.BlockSpec((rows, D_MODEL), lambda j: (0, 0)),
                  pl.BlockSpec((D_MODEL, tn), lambda j: (0, j)),
                  pl.BlockSpec((1, tn), lambda j: (0, j))],
        out_specs=pl.BlockSpec((rows, tn), lambda j: (0, j)),
        out_shape=jax.ShapeDtypeStruct((rows, n), F32),
        name="modulation",
    )(cvec, w_ada, b_ada)


def _store_heads(k_ref, vt_ref, k_all, vt_all):
    for hd in range(MLA_HEADS):
        k_ref[0, hd] = k_all[:, hd * HEAD_PAD:(hd + 1) * HEAD_PAD].astype(BF16)
        vt_ref[0, hd, :V_DIM, :] = vt_all[hd * V_DIM:(hd + 1) * V_DIM].astype(BF16)
        vt_ref[0, hd, V_DIM:, :] = jnp.ones((V_PAD - V_DIM, vt_all.shape[1]), BF16)


def _ctx_kernel(x_ref, mod_ref, w_ref, b_ref, kvg_ref, wk2_ref, wvt_ref, k_ref, vt_ref):
    y = _layer_norm_rows(x_ref[0])
    h = (y * (1.0 + mod_ref[0, 1:2, :]) + mod_ref[0, 0:1, :]).astype(BF16)
    proj = _dot(h, w_ref[...]) + b_ref[...]
    ckv_n = _rms_rows(proj[:, :KV_LORA], kvg_ref[...])
    ckv_b = ckv_n.astype(BF16)
    kin = jnp.concatenate([ckv_b, proj[:, KV_LORA:].astype(BF16)], axis=1)
    _store_heads(k_ref, vt_ref, _dot(kin, wk2_ref[...]), _dot_nt(wvt_ref[...], ckv_b))


def _ctx_proj(ctx, mod, mod_row, w_ctx, b_ctx, kvg, wk2, wvt):
    b, t, _ = ctx.shape
    const = lambda *shape: pl.BlockSpec(shape, lambda i: (0,) * len(shape))
    return pl.pallas_call(
        _ctx_kernel,
        grid=(b,),
        in_specs=[pl.BlockSpec((1, t, D_MODEL), lambda i: (i, 0, 0)),
                  pl.BlockSpec((1, 3, D_MODEL), lambda i: (mod_row, 0, 0)),
                  const(D_MODEL, 2 * LANES), const(1, 2 * LANES), const(1, KV_LORA),
                  const(2 * LANES, MLA_HEADS * HEAD_PAD), const(MLA_WIDTH, KV_LORA)],
        out_specs=[pl.BlockSpec((1, MLA_HEADS, t, HEAD_PAD), lambda i: (i, 0, 0, 0)),
                   pl.BlockSpec((1, MLA_HEADS, V_PAD, t), lambda i: (i, 0, 0, 0))],
        out_shape=[jax.ShapeDtypeStruct((b, MLA_HEADS, t, HEAD_PAD), BF16),
                   jax.ShapeDtypeStruct((b, MLA_HEADS, V_PAD, t), BF16)],
        compiler_params=pltpu.CompilerParams(vmem_limit_bytes=VMEM_LIMIT),
        name="ctx_proj",
    )(ctx, mod, w_ctx, b_ctx, kvg, wk2, wvt)


def _lat_kernel(x_ref, mod_ref, w_in_ref, b_in_ref, qg_ref, wqt_ref, kvg_ref, wk2_ref, wvt_ref, wc_ref,
                cosq_ref, sinq_ref, cosk_ref, sink_ref,
                qt_ref, k_ref, vt_ref, gates_ref, d_ref, h_scr):
    y = _layer_norm_rows(x_ref[0])
    h_scr[...] = (y * (1.0 + mod_ref[0, 1:2, :]) + mod_ref[0, 0:1, :]).astype(BF16)

    def proj(lo, hi):
        return _dot(h_scr[...], w_in_ref[:, lo:hi]) + b_in_ref[:, lo:hi]

    low = proj(COL_QLAT, COL_GMLA)

    qn = _rms_rows(low[:, COL_QLAT:COL_CKV], qg_ref[...]).astype(BF16)
    qt2 = _dot_nt(wqt_ref[...], qn)
    scale = QK_SCALE
    cosq = cosq_ref[...]
    sinq = sinq_ref[...]
    rot_base = MLA_HEADS * HEAD_PAD
    for hd in range(MLA_HEADS):
        base = hd * HEAD_PAD
        qt_ref[0, hd, :NOPE_DIM, :] = (qt2[base:base + NOPE_DIM] * scale).astype(BF16)
        rope = (qt2[base + NOPE_DIM:base + QK_DIM] * cosq
                + qt2[rot_base + hd * ROPE_DIM:rot_base + (hd + 1) * ROPE_DIM] * sinq)
        qt_ref[0, hd, NOPE_DIM:QK_DIM, :] = rope.astype(BF16)
        qt_ref[0, hd, QK_DIM:, :] = jnp.zeros((HEAD_PAD - QK_DIM, qt2.shape[1]), BF16)

    ckv_b = _rms_rows(low[:, COL_CKV:COL_KROPE], kvg_ref[...]).astype(BF16)
    kr = low[:, COL_KROPE:COL_GMLA]
    half = ROPE_DIM // 2
    rot = pltpu.roll(kr, half, axis=1) - pltpu.roll(kr, LANES - half, axis=1)
    kro = kr * cosk_ref[...] + rot * sink_ref[...]
    kin = jnp.concatenate([ckv_b, kro.astype(BF16)], axis=1)
    _store_heads(k_ref, vt_ref, _dot(kin, wk2_ref[...]), _dot_nt(wvt_ref[...], ckv_b))

    gates_ref[0, :, :MLA_WIDTH] = _silu(proj(COL_GMLA, COL_FIN)).astype(BF16)
    gates_ref[0, :, MLA_WIDTH:] = _silu(proj(COL_GF, D_IN_PAD)).astype(BF16)

    d_ref[0] = _dot(proj(COL_FIN, COL_GF).astype(BF16), wc_ref[...]).astype(BF16)


def _lat_proj(x, mod, w_in_p, b_in_p, qg, wqt, kvg, wk2, wvt, wc, cosq, sinq, cosk, sink, tm):
    b, s, _ = x.shape
    const = lambda *shape: pl.BlockSpec(shape, lambda i, j: (0,) * len(shape))
    hp = MLA_HEADS * HEAD_PAD
    return pl.pallas_call(
        _lat_kernel,
        grid=(b, s // tm),
        in_specs=[pl.BlockSpec((1, tm, D_MODEL), lambda i, j: (i, j, 0)),
                  pl.BlockSpec((1, 3, D_MODEL), lambda i, j: (i, 0, 0)),
                  const(D_MODEL, D_IN_PAD), const(1, D_IN_PAD), const(1, Q_LORA),
                  const(hp + MLA_HEADS * ROPE_DIM, Q_LORA), const(1, KV_LORA),
                  const(2 * LANES, hp), const(MLA_WIDTH, KV_LORA), const(F_WIDTH, 2 * F_WIDTH),
                  pl.BlockSpec((ROPE_DIM, tm), lambda i, j: (0, j)),
                  pl.BlockSpec((ROPE_DIM, tm), lambda i, j: (0, j)),
                  pl.BlockSpec((tm, LANES), lambda i, j: (j, 0)),
                  pl.BlockSpec((tm, LANES), lambda i, j: (j, 0))],
        out_specs=[pl.BlockSpec((1, MLA_HEADS, HEAD_PAD, tm), lambda i, j: (i, 0, 0, j)),
                   pl.BlockSpec((1, MLA_HEADS, tm, HEAD_PAD), lambda i, j: (i, 0, j, 0)),
                   pl.BlockSpec((1, MLA_HEADS, V_PAD, tm), lambda i, j: (i, 0, 0, j)),
                   pl.BlockSpec((1, tm, D_MIX), lambda i, j: (i, j, 0)),
                   pl.BlockSpec((1, tm, 2 * F_WIDTH), lambda i, j: (i, j, 0))],
        out_shape=[jax.ShapeDtypeStruct((b, MLA_HEADS, HEAD_PAD, s), BF16),
                   jax.ShapeDtypeStruct((b, MLA_HEADS, s, HEAD_PAD), BF16),
                   jax.ShapeDtypeStruct((b, MLA_HEADS, V_PAD, s), BF16),
                   jax.ShapeDtypeStruct((b, s, D_MIX), BF16),
                   jax.ShapeDtypeStruct((b, s, 2 * F_WIDTH), BF16)],
        scratch_shapes=[pltpu.VMEM((tm, D_MODEL), BF16)],
        compiler_params=pltpu.CompilerParams(vmem_limit_bytes=VMEM_LIMIT),
        name="lat_proj",
    )(x, mod, w_in_p, b_in_p, qg, wqt, kvg, wk2, wvt, wc, cosq, sinq, cosk, sink)


def _fft_kernel(dr_ref, di_ref, w1_ref, m3_ref, z_ref, xs, ys, zs):
    r = FFT_RADIX

    def fill(t1, carry):
        src = pl.multiple_of(t1 * r, r)
        dst = pl.multiple_of(t1 * X_PITCH, 8)
        re = dr_ref[0, pl.ds(src, r), :].astype(F32)
        im = di_ref[0, pl.ds(src, r), :].astype(F32)
        xs[0, pl.ds(dst, r), :] = re[:, :LANES]
        xs[1, pl.ds(dst, r), :] = re[:, LANES:]
        xs[2, pl.ds(dst, r), :] = im[:, :LANES]
        xs[3, pl.ds(dst, r), :] = im[:, LANES:]
        return carry

    lax.fori_loop(0, r, fill, 0)

    def stage1(g, carry):
        cols = []
        for u in range(FFT_UNROLL):
            t2 = g * FFT_UNROLL + u
            parts = [xs[sl, pl.ds(t2, r, stride=X_PITCH), :] for sl in range(4)]
            cols.append(jnp.concatenate([jnp.concatenate(parts[0:2], axis=1),
                                         jnp.concatenate(parts[2:4], axis=1)], axis=0))
        rhs = jnp.concatenate(cols, axis=1).astype(BF16)
        y = _dot(w1_ref[...], rhs)
        for u in range(FFT_UNROLL):
            dst = pl.multiple_of((g * FFT_UNROLL + u) * Y_PITCH, 8)
            ys[0, pl.ds(dst, 2 * r), :] = y[:, (2 * u) * LANES:(2 * u + 1) * LANES]
            ys[1, pl.ds(dst, 2 * r), :] = y[:, (2 * u + 1) * LANES:(2 * u + 2) * LANES]
        return carry

    lax.fori_loop(0, r // FFT_UNROLL, stage1, 0)

    def stage2(g, carry):
        for u in range(FFT_UNROLL):
            k1 = g * FFT_UNROLL + u
            yr = jnp.concatenate([ys[sl, pl.ds(k1, r, stride=Y_PITCH), :] for sl in range(2)], axis=1)
            yi = jnp.concatenate([ys[sl, pl.ds(r + k1, r, stride=Y_PITCH), :] for sl in range(2)], axis=1)
            rhs = jnp.concatenate([yr, yi], axis=0).astype(BF16)
            z = _dot(m3_ref[k1], rhs)
            zs[0, pl.ds(k1, r, stride=X_PITCH), :] = z[:, :LANES]
            zs[1, pl.ds(k1, r, stride=X_PITCH), :] = z[:, LANES:]
        return carry

    lax.fori_loop(0, r // FFT_UNROLL, stage2, 0)

    def drain(k2, carry):
        src = pl.multiple_of(k2 * X_PITCH, 8)
        dst = pl.multiple_of(k2 * r, r)
        z_ref[0, pl.ds(dst, r), :] = jnp.concatenate(
            [zs[0, pl.ds(src, r), :], zs[1, pl.ds(src, r), :]], axis=1).astype(BF16)
        return carry

    lax.fori_loop(0, r, drain, 0)


def _fft(d, w1, m3):
    b, s, _ = d.shape
    r = FFT_RADIX
    cw = 2 * LANES
    nblk = F_WIDTH // cw
    return pl.pallas_call(
        _fft_kernel,
        grid=(b, nblk),
        in_specs=[pl.BlockSpec((1, s, cw), lambda i, g: (i, 0, g)),
                  pl.BlockSpec((1, s, cw), lambda i, g: (i, 0, nblk + g)),
                  pl.BlockSpec((2 * r, 2 * r), lambda i, g: (0, 0)),
                  pl.BlockSpec((r, r, 2 * r), lambda i, g: (0, 0, 0))],
        out_specs=pl.BlockSpec((1, s, cw), lambda i, g: (i, 0, g)),
        out_shape=jax.ShapeDtypeStruct((b, s, F_WIDTH), BF16),
        scratch_shapes=[pltpu.VMEM((4, r * X_PITCH, LANES), F32),
                        pltpu.VMEM((2, r * Y_PITCH, LANES), F32),
                        pltpu.VMEM((2, r * X_PITCH, LANES), F32)],
        compiler_params=pltpu.CompilerParams(vmem_limit_bytes=VMEM_LIMIT),
        name="fft",
    )(d, d, w1, m3)


def _attn_kernel(qt_ref, kc_ref, kl_ref, vct_ref, vlt_ref, o_ref,
                 s_scr0, s_scr1, p_scr0, p_scr1, acc_scr, m_scr, o_scr, *, nq, sck):
    j = pl.program_id(1)
    tq = qt_ref.shape[3]
    n_ctx = kc_ref.shape[2]
    n_lat = kl_ref.shape[2]
    nh = MLA_HEADS
    n_body = 2
    lat_rows = n_lat // n_body
    ctx_rows = n_ctx // n_body

    def fold8(v, op):
        return op(v.reshape(v.shape[0] // 8, 8, tq), axis=0)

    s_scr = (s_scr0, s_scr1)
    p_scr = (p_scr0, p_scr1)

    def phase(h, s_qk, blk, do_qk, do_ex, do_pv):
        h_pv = (h + nh - 2) % nh
        s_ex = 1 - s_qk
        if do_ex:
            mb = jnp.broadcast_to(jnp.max(m_scr[s_ex], axis=0, keepdims=True), (8, tq))
        if do_pv:
            acc_scr[...] = jnp.zeros_like(acc_scr)

        def work(k_ref, src, row0, n, m8c):
            rows = pl.ds(row0, n)
            if do_ex:
                sv = s_scr[s_ex][rows, :]
                p = jnp.exp2(sv.reshape(n // 8, 8, tq) - mb[None]).reshape(n, tq)
                p_scr[s_ex][rows, :] = p.astype(BF16)
            if do_qk:
                s = _dot(k_ref[0, h, pl.ds(src, n), :], qt_ref[0, h])
                s_scr[s_qk][rows, :] = s
                m8c = jnp.maximum(m8c, fold8(s, jnp.max))
            return m8c

        def body(c, m8c):
            lat0 = pl.multiple_of(c * lat_rows, lat_rows)
            ctx0 = pl.multiple_of(c * ctx_rows, ctx_rows)
            for sub in range(lat_rows // sck):
                m8c = work(kl_ref, lat0 + sub * sck, lat0 + sub * sck, sck, m8c)
            m8c = work(kc_ref, ctx0, n_lat + ctx0, ctx_rows, m8c)
            if do_pv:
                acc_scr[...] += (
                    _dot(vlt_ref[0, h_pv, :, pl.ds(lat0, lat_rows)], p_scr[s_qk][pl.ds(lat0, lat_rows), :])
                    + _dot(vct_ref[0, h_pv, :, pl.ds(ctx0, ctx_rows)], p_scr[s_qk][pl.ds(n_lat + ctx0, ctx_rows), :]))
            return m8c

        m8 = lax.fori_loop(0, n_body, body, jnp.full((8, tq), -jnp.inf, F32))
        if do_pv:
            pv_blk = jnp.where(jnp.asarray(h) < 2, blk - 1, blk)
            acc = acc_scr[...]
            o_scr[pv_blk % 2, h_pv] = acc[:V_DIM] / acc[V_DIM:V_DIM + 1]
        if do_qk:
            m_scr[s_qk] = m8

    @pl.when(j == 0)
    def _():
        phase(0, 0, 0, True, False, False)
        phase(1, 1, 0, True, True, False)

    @pl.when(j < nq)
    def _():
        def run(pair, carry):
            phase(2 * pair, 0, j, True, True, True)
            phase(2 * pair + 1, 1, j, True, True, True)
            return carry
        lax.fori_loop(jnp.where(j == 0, 1, 0), nh // 2, run, 0)

    @pl.when(j == nq)
    def _():
        phase(0, 0, nq, False, True, True)
        phase(1, 1, nq, False, False, True)

    @pl.when(j > 0)
    def _():
        done = o_scr[(j - 1) % 2]
        o_ref[0] = done.reshape(nh * V_DIM, tq).T.astype(BF16)


def _attention(qt, kc, kl, vct, vlt, tq, sck):
    b, nh, hp, s = qt.shape
    n_ctx = kc.shape[2]
    nq = s // tq
    once = dict(pipeline_mode=pl.Buffered(1))
    return pl.pallas_call(
        functools.partial(_attn_kernel, nq=nq, sck=sck),
        grid=(b, nq + 1),
        in_specs=[pl.BlockSpec((1, nh, hp, tq), lambda i, j: (i, 0, 0, jnp.minimum(j, nq - 1))),
                  pl.BlockSpec((1, nh, n_ctx, hp), lambda i, j: (i, 0, 0, 0), **once),
                  pl.BlockSpec((1, nh, s, hp), lambda i, j: (i, 0, 0, 0), **once),
                  pl.BlockSpec((1, nh, V_PAD, n_ctx), lambda i, j: (i, 0, 0, 0), **once),
                  pl.BlockSpec((1, nh, V_PAD, s), lambda i, j: (i, 0, 0, 0), **once)],
        out_specs=pl.BlockSpec((1, tq, MLA_WIDTH), lambda i, j: (i, jnp.maximum(j - 1, 0), 0)),
        out_shape=jax.ShapeDtypeStruct((b, s, MLA_WIDTH), BF16),
        scratch_shapes=[pltpu.VMEM((n_ctx + s, tq), F32), pltpu.VMEM((n_ctx + s, tq), F32),
                        pltpu.VMEM((n_ctx + s, tq), BF16), pltpu.VMEM((n_ctx + s, tq), BF16),
                        pltpu.VMEM((V_PAD, tq), F32),
                        pltpu.VMEM((2, 8, tq), F32),
                        pltpu.VMEM((2, nh, V_DIM, tq), F32)],
        compiler_params=pltpu.CompilerParams(vmem_limit_bytes=VMEM_LIMIT,
                                             dimension_semantics=("arbitrary", "arbitrary")),
        name="attention",
    )(qt, kc, kl, vct, vlt)


def _merge_kernel(x_ref, mod_ref, attn_ref, z_ref, gates_ref, wf_ref, bf_ref, wo_ref, bo_ref, g_ref, b_ref, o_ref):
    four = _dot(z_ref[0], wf_ref[...]) + bf_ref[...]
    gates = gates_ref[0]
    ya = attn_ref[0].astype(F32) * gates[:, :MLA_WIDTH].astype(F32)
    yf = four * gates[:, MLA_WIDTH:].astype(F32)
    ycat = jnp.concatenate([ya.astype(BF16), yf.astype(BF16)], axis=1)
    y = _dot(ycat, wo_ref[...]) + bo_ref[...]
    r = DEEPNORM_ALPHA * x_ref[0] + mod_ref[0, 2:3, :] * y
    o_ref[0] = _layer_norm_rows(r) * g_ref[...] + b_ref[...]


def _merge(x, mod, attn, z, gates, wf, bf, wo, bo, g, bb, tm):
    b, s, _ = x.shape
    const = lambda *shape: pl.BlockSpec(shape, lambda i, j: (0,) * len(shape))
    tok = lambda w: pl.BlockSpec((1, tm, w), lambda i, j: (i, j, 0))
    return pl.pallas_call(
        _merge_kernel,
        grid=(b, s // tm),
        in_specs=[tok(D_MODEL), pl.BlockSpec((1, 3, D_MODEL), lambda i, j: (i, 0, 0)),
                  tok(MLA_WIDTH), tok(F_WIDTH), tok(D_MIX),
                  const(F_WIDTH, F_WIDTH), const(1, F_WIDTH), const(D_MIX, D_MODEL), const(1, D_MODEL),
                  const(1, D_MODEL), const(1, D_MODEL)],
        out_specs=tok(D_MODEL),
        out_shape=jax.ShapeDtypeStruct((b, s, D_MODEL), x.dtype),
        compiler_params=pltpu.CompilerParams(vmem_limit_bytes=VMEM_LIMIT),
        name="merge",
    )(x, mod, attn, z, gates, wf, bf, wo, bo, g, bb)


def _rot_cols(w):
    half = w.shape[-1] // 2
    return jnp.concatenate([-w[..., half:], w[..., :half]], axis=-1)


def _rope_tables(s):
    n_rows = s // GRID_W
    pos = np.arange(s)
    rows = (pos // GRID_W).astype(np.float64)
    cols = (pos % GRID_W).astype(np.float64)
    assert n_rows * GRID_W == s
    axis_dim = ROPE_DIM // 2
    inv_freq = ROPE_BASE ** (-np.arange(0, axis_dim, 2, dtype=np.float64) / axis_dim)
    ang = np.concatenate([rows[:, None] * inv_freq, cols[:, None] * inv_freq], axis=-1)
    ang = np.concatenate([ang, ang], axis=-1)
    return np.cos(ang), np.sin(ang)


def _dft_constants():
    r = FFT_RADIX
    n = r * r
    k = np.arange(r, dtype=np.float64)
    ang1 = 2.0 * np.pi * np.outer(k, k) / r
    c1, s1 = np.cos(ang1), np.sin(ang1)
    w1 = np.block([[c1, s1], [-s1, c1]])
    k1 = np.arange(r)[:, None, None]
    k2 = np.arange(r)[None, :, None]
    t2 = np.arange(r)[None, None, :]
    ang3 = 2.0 * np.pi * t2 * (k1 + r * k2) / n
    m3 = np.concatenate([np.cos(ang3), np.sin(ang3)], axis=-1) / r
    c = np.arange(F_GROUP_DIM, dtype=np.float64)
    angc = 2.0 * np.pi * np.outer(c, c) / F_GROUP_DIM
    norm = 1.0 / math.sqrt(F_GROUP_DIM)
    eye = np.eye(F_GROUPS)
    wc = np.concatenate([np.kron(eye, np.cos(angc) * norm), -np.kron(eye, np.sin(angc) * norm)], axis=1)
    return w1, m3, wc


def kernel(x, c, ctx, c_ctx, w_ada, b_ada, w_in, b_in, q_norm_g, w_q_up, kv_norm_g, w_kv_up, w_fourier,
           b_fourier, w_out, b_out, post_ln_g, post_ln_b):
    b, s, _ = x.shape
    assert s == FFT_RADIX * FFT_RADIX and w_ada.shape[0] == 1
    l = 0

    wi, bi = w_in[l], b_in[l]
    o_q, o_kv, o_kr, o_gm, o_f, o_gf = 0, Q_LORA, Q_LORA + KV_LORA, Q_LORA + KV_LORA + ROPE_DIM, \
        Q_LORA + KV_LORA + ROPE_DIM + MLA_WIDTH, Q_LORA + KV_LORA + ROPE_DIM + MLA_WIDTH + F_WIDTH

    def in_cols(a):
        kr = a[..., o_kr:o_gm]
        zpad = jnp.zeros(a.shape[:-1] + (LANES - ROPE_DIM,), a.dtype)
        return jnp.concatenate([a[..., o_q:o_kr], kr, zpad, a[..., o_gm:]], axis=-1)

    w_in_p = in_cols(wi).astype(BF16)
    b_in_p = in_cols(bi)[None, :]
    w_ctx = w_in_p[:, COL_CKV:COL_GMLA]
    b_ctx = b_in_p[:, COL_CKV:COL_GMLA]

    wq = w_q_up[l].reshape(Q_LORA, MLA_HEADS, QK_DIM)
    wq_pad = jnp.pad(wq, ((0, 0), (0, 0), (0, HEAD_PAD - QK_DIM))).reshape(Q_LORA, MLA_HEADS * HEAD_PAD)
    wq_rot = _rot_cols(wq[..., NOPE_DIM:]).reshape(Q_LORA, MLA_HEADS * ROPE_DIM)
    wqt = jnp.concatenate([wq_pad, wq_rot], axis=1).T.astype(BF16)

    wkv = w_kv_up[l].reshape(KV_LORA, MLA_HEADS, NOPE_DIM + V_DIM)
    wk_pad = jnp.pad(wkv[..., :NOPE_DIM], ((0, 0), (0, 0), (0, HEAD_PAD - NOPE_DIM)))
    wk_pad = wk_pad.reshape(KV_LORA, MLA_HEADS * HEAD_PAD)
    place = np.zeros((LANES, MLA_HEADS, HEAD_PAD), np.float32)
    for j in range(ROPE_DIM):
        place[j, :, NOPE_DIM + j] = 1.0
    wk2 = jnp.concatenate([wk_pad, jnp.asarray(place.reshape(LANES, -1))], axis=0).astype(BF16)
    wvt = wkv[..., NOPE_DIM:].reshape(KV_LORA, MLA_WIDTH).T.astype(BF16)

    w1_np, m3_np, wc_np = _dft_constants()
    w1 = jnp.asarray(w1_np, F32).astype(BF16)
    m3 = jnp.asarray(m3_np, F32).astype(BF16)
    wc = jnp.asarray(wc_np, F32).astype(BF16)

    cos_np, sin_np = _rope_tables(s)
    scale = QK_SCALE
    cosq = jnp.asarray((cos_np * scale).T, F32)
    sinq = jnp.asarray((sin_np * scale).T, F32)
    kpad = np.zeros((s, LANES - ROPE_DIM))
    cosk = jnp.asarray(np.concatenate([cos_np, kpad], axis=1), F32)
    sink = jnp.asarray(np.concatenate([sin_np, kpad], axis=1), F32)

    qg = q_norm_g[l][None, :]
    kvg = kv_norm_g[l][None, :]

    rows = ((b + 1 + 7) // 8) * 8
    cvec = jnp.concatenate([c, c_ctx[None, :], jnp.zeros((rows - b - 1, D_MODEL), c.dtype)], axis=0)
    mod = _modulation(cvec, w_ada[l], b_ada[l][None, :]).reshape(rows, 3, D_MODEL)

    kc, vct = _ctx_proj(ctx, mod, b, w_ctx, b_ctx, kvg, wk2, wvt)
    qt, kl, vlt, gates, d = _lat_proj(x, mod, w_in_p, b_in_p, qg, wqt, kvg, wk2, wvt, wc,
                                      cosq, sinq, cosk, sink, tm=512)
    z = _fft(d, w1, m3)
    attn = _attention(qt, kc, kl, vct, vlt, tq=512, sck=128)
    return _merge(x, mod, attn, z, gates, w_fourier[l].astype(BF16), b_fourier[l][None, :],
                  w_out[l].astype(BF16), b_out[l][None, :], post_ln_g[l][None, :], post_ln_b[l][None, :], tm=512)
```

```python
import functools
import math

import jax
import jax.numpy as jnp
import numpy as np
from jax import lax
from jax.experimental import pallas as pl
from jax.experimental.pallas import tpu as pltpu

D_MODEL = 1024
CTX_LEN = 256
GRID_W = 64
MLA_HEADS = 8
NOPE_DIM = 64
ROPE_DIM = 32
V_DIM = 64
QK_DIM = NOPE_DIM + ROPE_DIM
Q_LORA = 256
KV_LORA = 128
MLA_WIDTH = MLA_HEADS * V_DIM
F_GROUPS = 4
F_GROUP_DIM = 128
F_WIDTH = F_GROUPS * F_GROUP_DIM
D_MIX = MLA_WIDTH + F_WIDTH
ROPE_BASE = 10000.0
LN_EPS = 1e-6
DEPTH = 1
DEEPNORM_ALPHA = (2.0 * DEPTH) ** 0.25
QK_SCALE = math.log2(math.e) / math.sqrt(QK_DIM)

LANES = 128
HEAD_PAD = 128
V_PAD = 80
FFT_RADIX = 64
X_PITCH = 72
Y_PITCH = 136
FFT_UNROLL = 16

COL_QLAT = 0
COL_CKV = 256
COL_KROPE = 384
COL_GMLA = 512
COL_FIN = 1024
COL_GF = 1536
D_IN_PAD = 2048

BF16 = jnp.bfloat16
F32 = jnp.float32

V7X_VMEM_BYTES = 64 * 1024 * 1024
VMEM_LIMIT = V7X_VMEM_BYTES - 8 * 1024 * 1024


def _dot(a, b):
    return jnp.dot(a, b, preferred_element_type=F32)


def _dot_nt(a, b):
    return lax.dot_general(a, b, (((1,), (1,)), ((), ())), preferred_element_type=F32)


def _layer_norm_rows(x):
    mu = jnp.mean(x, axis=-1, keepdims=True)
    xc = x - mu
    var = jnp.mean(xc * xc, axis=-1, keepdims=True)
    return xc * lax.rsqrt(var + LN_EPS)


def _rms_rows(x, g):
    return x * lax.rsqrt(jnp.mean(x * x, axis=-1, keepdims=True) + LN_EPS) * g


def _silu(x):
    return x * jax.nn.sigmoid(x)


def _mod_kernel(c_ref, w_ref, b_ref, o_ref):
    a = _silu(c_ref[...]).astype(BF16)
    o_ref[...] = _dot(a, w_ref[...].astype(BF16)) + b_ref[...]


def _modulation(cvec, w_ada, b_ada):
    rows = cvec.shape[0]
    n = w_ada.shape[1]
    tn = 1024
    return pl.pallas_call(
        _mod_kernel,
        grid=(n // tn,),
        in_specs=[pl.BlockSpec((rows, D_MODEL), lambda j: (0, 0)),
                  pl.BlockSpec((D_MODEL, tn), lambda j: (0, j)),
                  pl.BlockSpec((1, tn), lambda j: (0, j))],
        out_specs=pl.BlockSpec((rows, tn), lambda j: (0, j)),
        out_shape=jax.ShapeDtypeStruct((rows, n), F32),
        name="modulation",
    )(cvec, w_ada, b_ada)


def _store_heads(k_ref, vt_ref, k_all, vt_all):
    for hd in range(MLA_HEADS):
        k_ref[0, hd] = k_all[:, hd * HEAD_PAD:(hd + 1) * HEAD_PAD].astype(BF16)
        vt_ref[0, hd, :V_DIM, :] = vt_all[hd * V_DIM:(hd + 1) * V_DIM].astype(BF16)
        vt_ref[0, hd, V_DIM:, :] = jnp.ones((V_PAD - V_DIM, vt_all.shape[1]), BF16)


def _ctx_kernel(x_ref, mod_ref, w_ref, b_ref, kvg_ref, wk2_ref, wvt_ref, k_ref, vt_ref):
    y = _layer_norm_rows(x_ref[0])
    h = (y * (1.0 + mod_ref[0, 1:2, :]) + mod_ref[0, 0:1, :]).astype(BF16)
    proj = _dot(h, w_ref[...]) + b_ref[...]
    ckv_n = _rms_rows(proj[:, :KV_LORA], kvg_ref[...])
    ckv_b = ckv_n.astype(BF16)
    kin = jnp.concatenate([ckv_b, proj[:, KV_LORA:].astype(BF16)], axis=1)
    _store_heads(k_ref, vt_ref, _dot(kin, wk2_ref[...]), _dot_nt(wvt_ref[...], ckv_b))


def _ctx_proj(ctx, mod, mod_row, w_ctx, b_ctx, kvg, wk2, wvt):
    b, t, _ = ctx.shape
    const = lambda *shape: pl.BlockSpec(shape, lambda i: (0,) * len(shape))
    return pl.pallas_call(
        _ctx_kernel,
        grid=(b,),
        in_specs=[pl.BlockSpec((1, t, D_MODEL), lambda i: (i, 0, 0)),
                  pl.BlockSpec((1, 3, D_MODEL), lambda i: (mod_row, 0, 0)),
                  const(D_MODEL, 2 * LANES), const(1, 2 * LANES), const(1, KV_LORA),
                  const(2 * LANES, MLA_HEADS * HEAD_PAD), const(MLA_WIDTH, KV_LORA)],
        out_specs=[pl.BlockSpec((1, MLA_HEADS, t, HEAD_PAD), lambda i: (i, 0, 0, 0)),
                   pl.BlockSpec((1, MLA_HEADS, V_PAD, t), lambda i: (i, 0, 0, 0))],
        out_shape=[jax.ShapeDtypeStruct((b, MLA_HEADS, t, HEAD_PAD), BF16),
                   jax.ShapeDtypeStruct((b, MLA_HEADS, V_PAD, t), BF16)],
        compiler_params=pltpu.CompilerParams(vmem_limit_bytes=VMEM_LIMIT),
        name="ctx_proj",
    )(ctx, mod, w_ctx, b_ctx, kvg, wk2, wvt)


def _lat_kernel(x_ref, mod_ref, w_in_ref, b_in_ref, qg_ref, wqt_ref, kvg_ref, wk2_ref, wvt_ref, wc_ref,
                cosq_ref, sinq_ref, cosk_ref, sink_ref,
                qt_ref, k_ref, vt_ref, gates_ref, d_ref, h_scr):
    y = _layer_norm_rows(x_ref[0])
    h_scr[...] = (y * (1.0 + mod_ref[0, 1:2, :]) + mod_ref[0, 0:1, :]).astype(BF16)

    def proj(lo, hi):
        return _dot(h_scr[...], w_in_ref[:, lo:hi]) + b_in_ref[:, lo:hi]

    low = proj(COL_QLAT, COL_GMLA)

    qn = _rms_rows(low[:, COL_QLAT:COL_CKV], qg_ref[...]).astype(BF16)
    qt2 = _dot_nt(wqt_ref[...], qn)
    scale = QK_SCALE
    cosq = cosq_ref[...]
    sinq = sinq_ref[...]
    rot_base = MLA_HEADS * HEAD_PAD
    for hd in range(MLA_HEADS):
        base = hd * HEAD_PAD
        qt_ref[0, hd, :NOPE_DIM, :] = (qt2[base:base + NOPE_DIM] * scale).astype(BF16)
        rope = (qt2[base + NOPE_DIM:base + QK_DIM] * cosq
                + qt2[rot_base + hd * ROPE_DIM:rot_base + (hd + 1) * ROPE_DIM] * sinq)
        qt_ref[0, hd, NOPE_DIM:QK_DIM, :] = rope.astype(BF16)
        qt_ref[0, hd, QK_DIM:, :] = jnp.zeros((HEAD_PAD - QK_DIM, qt2.shape[1]), BF16)

    ckv_b = _rms_rows(low[:, COL_CKV:COL_KROPE], kvg_ref[...]).astype(BF16)
    kr = low[:, COL_KROPE:COL_GMLA]
    half = ROPE_DIM // 2
    rot = pltpu.roll(kr, half, axis=1) - pltpu.roll(kr, LANES - half, axis=1)
    kro = kr * cosk_ref[...] + rot * sink_ref[...]
    kin = jnp.concatenate([ckv_b, kro.astype(BF16)], axis=1)
    _store_heads(k_ref, vt_ref, _dot(kin, wk2_ref[...]), _dot_nt(wvt_ref[...], ckv_b))

    gates_ref[0, :, :MLA_WIDTH] = _silu(proj(COL_GMLA, COL_FIN)).astype(BF16)
    gates_ref[0, :, MLA_WIDTH:] = _silu(proj(COL_GF, D_IN_PAD)).astype(BF16)

    d_ref[0] = _dot(proj(COL_FIN, COL_GF).astype(BF16), wc_ref[...]).astype(BF16)


def _lat_proj(x, mod, w_in_p, b_in_p, qg, wqt, kvg, wk2, wvt, wc, cosq, sinq, cosk, sink, tm):
    b, s, _ = x.shape
    const = lambda *shape: pl.BlockSpec(shape, lambda i, j: (0,) * len(shape))
    hp = MLA_HEADS * HEAD_PAD
    return pl.pallas_call(
        _lat_kernel,
        grid=(b, s // tm),
        in_specs=[pl.BlockSpec((1, tm, D_MODEL), lambda i, j: (i, j, 0)),
                  pl.BlockSpec((1, 3, D_MODEL), lambda i, j: (i, 0, 0)),
                  const(D_MODEL, D_IN_PAD), const(1, D_IN_PAD), const(1, Q_LORA),
                  const(hp + MLA_HEADS * ROPE_DIM, Q_LORA), const(1, KV_LORA),
                  const(2 * LANES, hp), const(MLA_WIDTH, KV_LORA), const(F_WIDTH, 2 * F_WIDTH),
                  pl.BlockSpec((ROPE_DIM, tm), lambda i, j: (0, j)),
                  pl.BlockSpec((ROPE_DIM, tm), lambda i, j: (0, j)),
                  pl.BlockSpec((tm, LANES), lambda i, j: (j, 0)),
                  pl.BlockSpec((tm, LANES), lambda i, j: (j, 0))],
        out_specs=[pl.BlockSpec((1, MLA_HEADS, HEAD_PAD, tm), lambda i, j: (i, 0, 0, j)),
                   pl.BlockSpec((1, MLA_HEADS, tm, HEAD_PAD), lambda i, j: (i, 0, j, 0)),
                   pl.BlockSpec((1, MLA_HEADS, V_PAD, tm), lambda i, j: (i, 0, 0, j)),
                   pl.BlockSpec((1, tm, D_MIX), lambda i, j: (i, j, 0)),
                   pl.BlockSpec((1, tm, 2 * F_WIDTH), lambda i, j: (i, j, 0))],
        out_shape=[jax.ShapeDtypeStruct((b, MLA_HEADS, HEAD_PAD, s), BF16),
                   jax.ShapeDtypeStruct((b, MLA_HEADS, s, HEAD_PAD), BF16),
                   jax.ShapeDtypeStruct((b, MLA_HEADS, V_PAD, s), BF16),
                   jax.ShapeDtypeStruct((b, s, D_MIX), BF16),
                   jax.ShapeDtypeStruct((b, s, 2 * F_WIDTH), BF16)],
        scratch_shapes=[pltpu.VMEM((tm, D_MODEL), BF16)],
        compiler_params=pltpu.CompilerParams(vmem_limit_bytes=VMEM_LIMIT),
        name="lat_proj",
    )(x, mod, w_in_p, b_in_p, qg, wqt, kvg, wk2, wvt, wc, cosq, sinq, cosk, sink)


def _fft_kernel(dr_ref, di_ref, w1_ref, m3_ref, z_ref, xs, ys, zs):
    r = FFT_RADIX

    def fill(t1, carry):
        src = pl.multiple_of(t1 * r, r)
        dst = pl.multiple_of(t1 * X_PITCH, 8)
        re = dr_ref[0, pl.ds(src, r), :].astype(F32)
        im = di_ref[0, pl.ds(src, r), :].astype(F32)
        xs[0, pl.ds(dst, r), :] = re[:, :LANES]
        xs[1, pl.ds(dst, r), :] = re[:, LANES:]
        xs[2, pl.ds(dst, r), :] = im[:, :LANES]
        xs[3, pl.ds(dst, r), :] = im[:, LANES:]
        return carry

    lax.fori_loop(0, r, fill, 0)

    def stage1(g, carry):
        cols = []
        for u in range(FFT_UNROLL):
            t2 = g * FFT_UNROLL + u
            parts = [xs[sl, pl.ds(t2, r, stride=X_PITCH), :] for sl in range(4)]
            cols.append(jnp.concatenate([jnp.concatenate(parts[0:2], axis=1),
                                         jnp.concatenate(parts[2:4], axis=1)], axis=0))
        rhs = jnp.concatenate(cols, axis=1).astype(BF16)
        y = _dot(w1_ref[...], rhs)
        for u in range(FFT_UNROLL):
            dst = pl.multiple_of((g * FFT_UNROLL + u) * Y_PITCH, 8)
            ys[0, pl.ds(dst, 2 * r), :] = y[:, (2 * u) * LANES:(2 * u + 1) * LANES]
            ys[1, pl.ds(dst, 2 * r), :] = y[:, (2 * u + 1) * LANES:(2 * u + 2) * LANES]
        return carry

    lax.fori_loop(0, r // FFT_UNROLL, stage1, 0)

    def stage2(g, carry):
        for u in range(FFT_UNROLL):
            k1 = g * FFT_UNROLL + u
            yr = jnp.concatenate([ys[sl, pl.ds(k1, r, stride=Y_PITCH), :] for sl in range(2)], axis=1)
            yi = jnp.concatenate([ys[sl, pl.ds(r + k1, r, stride=Y_PITCH), :] for sl in range(2)], axis=1)
            rhs = jnp.concatenate([yr, yi], axis=0).astype(BF16)
            z = _dot(m3_ref[k1], rhs)
            zs[0, pl.ds(k1, r, stride=X_PITCH), :] = z[:, :LANES]
            zs[1, pl.ds(k1, r, stride=X_PITCH), :] = z[:, LANES:]
        return carry

    lax.fori_loop(0, r // FFT_UNROLL, stage2, 0)

    def drain(k2, carry):
        src = pl.multiple_of(k2 * X_PITCH, 8)
        dst = pl.multiple_of(k2 * r, r)
        z_ref[0, pl.ds(dst, r), :] = jnp.concatenate(
            [zs[0, pl.ds(src, r), :], zs[1, pl.ds(src, r), :]], axis=1).astype(BF16)
        return carry

    lax.fori_loop(0, r, drain, 0)


def _fft(d, w1, m3):
    b, s, _ = d.shape
    r = FFT_RADIX
    cw = 2 * LANES
    nblk = F_WIDTH // cw
    return pl.pallas_call(
        _fft_kernel,
        grid=(b, nblk),
        in_specs=[pl.BlockSpec((1, s, cw), lambda i, g: (i, 0, g)),
                  pl.BlockSpec((1, s, cw), lambda i, g: (i, 0, nblk + g)),
                  pl.BlockSpec((2 * r, 2 * r), lambda i, g: (0, 0)),
                  pl.BlockSpec((r, r, 2 * r), lambda i, g: (0, 0, 0))],
        out_specs=pl.BlockSpec((1, s, cw), lambda i, g: (i, 0, g)),
        out_shape=jax.ShapeDtypeStruct((b, s, F_WIDTH), BF16),
        scratch_shapes=[pltpu.VMEM((4, r * X_PITCH, LANES), F32),
                        pltpu.VMEM((2, r * Y_PITCH, LANES), F32),
                        pltpu.VMEM((2, r * X_PITCH, LANES), F32)],
        compiler_params=pltpu.CompilerParams(vmem_limit_bytes=VMEM_LIMIT),
        name="fft",
    )(d, d, w1, m3)


def _attn_kernel(qt_ref, kc_ref, kl_ref, vct_ref, vlt_ref, o_ref,
                 s_scr0, s_scr1, p_scr0, p_scr1, acc_scr, m_scr, o_scr, *, nq, sck):
    j = pl.program_id(1)
    tq = qt_ref.shape[3]
    n_ctx = kc_ref.shape[2]
    n_lat = kl_ref.shape[2]
    nh = MLA_HEADS
    n_body = 2
    lat_rows = n_lat // n_body
    ctx_rows = n_ctx // n_body

    def fold8(v, op):
        return op(v.reshape(v.shape[0] // 8, 8, tq), axis=0)

    s_scr = (s_scr0, s_scr1)
    p_scr = (p_scr0, p_scr1)

    def phase(h, s_qk, blk, do_qk, do_ex, do_pv):
        h_pv = (h + nh - 2) % nh
        s_ex = 1 - s_qk
        if do_ex:
            mb = jnp.broadcast_to(jnp.max(m_scr[s_ex], axis=0, keepdims=True), (8, tq))
        if do_pv:
            acc_scr[...] = jnp.zeros_like(acc_scr)

        def work(k_ref, src, row0, n, m8c):
            rows = pl.ds(row0, n)
            if do_ex:
                sv = s_scr[s_ex][rows, :]
                p = jnp.exp2(sv.reshape(n // 8, 8, tq) - mb[None]).reshape(n, tq)
                p_scr[s_ex][rows, :] = p.astype(BF16)
            if do_qk:
                s = _dot(k_ref[0, h, pl.ds(src, n), :], qt_ref[0, h])
                s_scr[s_qk][rows, :] = s
                m8c = jnp.maximum(m8c, fold8(s, jnp.max))
            return m8c

        def body(c, m8c):
            lat0 = pl.multiple_of(c * lat_rows, lat_rows)
            ctx0 = pl.multiple_of(c * ctx_rows, ctx_rows)
            for sub in range(lat_rows // sck):
                m8c = work(kl_ref, lat0 + sub * sck, lat0 + sub * sck, sck, m8c)
            m8c = work(kc_ref, ctx0, n_lat + ctx0, ctx_rows, m8c)
            if do_pv:
                acc_scr[...] += (
                    _dot(vlt_ref[0, h_pv, :, pl.ds(lat0, lat_rows)], p_scr[s_qk][pl.ds(lat0, lat_rows), :])
                    + _dot(vct_ref[0, h_pv, :, pl.ds(ctx0, ctx_rows)], p_scr[s_qk][pl.ds(n_lat + ctx0, ctx_rows), :]))
            return m8c

        m8 = lax.fori_loop(0, n_body, body, jnp.full((8, tq), -jnp.inf, F32))
        if do_pv:
            pv_blk = jnp.where(jnp.asarray(h) < 2, blk - 1, blk)
            acc = acc_scr[...]
            o_scr[pv_blk % 2, h_pv] = acc[:V_DIM] / acc[V_DIM:V_DIM + 1]
        if do_qk:
            m_scr[s_qk] = m8

    @pl.when(j == 0)
    def _():
        phase(0, 0, 0, True, False, False)
        phase(1, 1, 0, True, True, False)

    @pl.when(j < nq)
    def _():
        def run(pair, carry):
            phase(2 * pair, 0, j, True, True, True)
            phase(2 * pair + 1, 1, j, True, True, True)
            return carry
        lax.fori_loop(jnp.where(j == 0, 1, 0), nh // 2, run, 0)

    @pl.when(j == nq)
    def _():
        phase(0, 0, nq, False, True, True)
        phase(1, 1, nq, False, False, True)

    @pl.when(j > 0)
    def _():
        done = o_scr[(j - 1) % 2]
        o_ref[0] = done.reshape(nh * V_DIM, tq).T.astype(BF16)


def _attention(qt, kc, kl, vct, vlt, tq, sck):
    b, nh, hp, s = qt.shape
    n_ctx = kc.shape[2]
    nq = s // tq
    once = dict(pipeline_mode=pl.Buffered(1))
    return pl.pallas_call(
        functools.partial(_attn_kernel, nq=nq, sck=sck),
        grid=(b, nq + 1),
        in_specs=[pl.BlockSpec((1, nh, hp, tq), lambda i, j: (i, 0, 0, jnp.minimum(j, nq - 1))),
                  pl.BlockSpec((1, nh, n_ctx, hp), lambda i, j: (i, 0, 0, 0), **once),
                  pl.BlockSpec((1, nh, s, hp), lambda i, j: (i, 0, 0, 0), **once),
                  pl.BlockSpec((1, nh, V_PAD, n_ctx), lambda i, j: (i, 0, 0, 0), **once),
                  pl.BlockSpec((1, nh, V_PAD, s), lambda i, j: (i, 0, 0, 0), **once)],
        out_specs=pl.BlockSpec((1, tq, MLA_WIDTH), lambda i, j: (i, jnp.maximum(j - 1, 0), 0)),
        out_shape=jax.ShapeDtypeStruct((b, s, MLA_WIDTH), BF16),
        scratch_shapes=[pltpu.VMEM((n_ctx + s, tq), F32), pltpu.VMEM((n_ctx + s, tq), F32),
                        pltpu.VMEM((n_ctx + s, tq), BF16), pltpu.VMEM((n_ctx + s, tq), BF16),
                        pltpu.VMEM((V_PAD, tq), F32),
                        pltpu.VMEM((2, 8, tq), F32),
                        pltpu.VMEM((2, nh, V_DIM, tq), F32)],
        compiler_params=pltpu.CompilerParams(vmem_limit_bytes=VMEM_LIMIT,
                                             dimension_semantics=("arbitrary", "arbitrary")),
        name="attention",
    )(qt, kc, kl, vct, vlt)


def _merge_kernel(x_ref, mod_ref, attn_ref, z_ref, gates_ref, wf_ref, bf_ref, wo_ref, bo_ref, g_ref, b_ref, o_ref):
    four = _dot(z_ref[0], wf_ref[...]) + bf_ref[...]
    gates = gates_ref[0]
    ya = attn_ref[0].astype(F32) * gates[:, :MLA_WIDTH].astype(F32)
    yf = four * gates[:, MLA_WIDTH:].astype(F32)
    ycat = jnp.concatenate([ya.astype(BF16), yf.astype(BF16)], axis=1)
    y = _dot(ycat, wo_ref[...]) + bo_ref[...]
    r = DEEPNORM_ALPHA * x_ref[0] + mod_ref[0, 2:3, :] * y
    o_ref[0] = _layer_norm_rows(r) * g_ref[...] + b_ref[...]


def _merge(x, mod, attn, z, gates, wf, bf, wo, bo, g, bb, tm):
    b, s, _ = x.shape
    const = lambda *shape: pl.BlockSpec(shape, lambda i, j: (0,) * len(shape))
    tok = lambda w: pl.BlockSpec((1, tm, w), lambda i, j: (i, j, 0))
    return pl.pallas_call(
        _merge_kernel,
        grid=(b, s // tm),
        in_specs=[tok(D_MODEL), pl.BlockSpec((1, 3, D_MODEL), lambda i, j: (i, 0, 0)),
                  tok(MLA_WIDTH), tok(F_WIDTH), tok(D_MIX),
                  const(F_WIDTH, F_WIDTH), const(1, F_WIDTH), const(D_MIX, D_MODEL), const(1, D_MODEL),
                  const(1, D_MODEL), const(1, D_MODEL)],
        out_specs=tok(D_MODEL),
        out_shape=jax.ShapeDtypeStruct((b, s, D_MODEL), x.dtype),
        compiler_params=pltpu.CompilerParams(vmem_limit_bytes=VMEM_LIMIT),
        name="merge",
    )(x, mod, attn, z, gates, wf, bf, wo, bo, g, bb)


def _rot_cols(w):
    half = w.shape[-1] // 2
    return jnp.concatenate([-w[..., half:], w[..., :half]], axis=-1)


def _rope_tables(s):
    n_rows = s // GRID_W
    pos = np.arange(s)
    rows = (pos // GRID_W).astype(np.float64)
    cols = (pos % GRID_W).astype(np.float64)
    assert n_rows * GRID_W == s
    axis_dim = ROPE_DIM // 2
    inv_freq = ROPE_BASE ** (-np.arange(0, axis_dim, 2, dtype=np.float64) / axis_dim)
    ang = np.concatenate([rows[:, None] * inv_freq, cols[:, None] * inv_freq], axis=-1)
    ang = np.concatenate([ang, ang], axis=-1)
    return np.cos(ang), np.sin(ang)


def _dft_constants():
    r = FFT_RADIX
    n = r * r
    k = np.arange(r, dtype=np.float64)
    ang1 = 2.0 * np.pi * np.outer(k, k) / r
    c1, s1 = np.cos(ang1), np.sin(ang1)
    w1 = np.block([[c1, s1], [-s1, c1]])
    k1 = np.arange(r)[:, None, None]
    k2 = np.arange(r)[None, :, None]
    t2 = np.arange(r)[None, None, :]
    ang3 = 2.0 * np.pi * t2 * (k1 + r * k2) / n
    m3 = np.concatenate([np.cos(ang3), np.sin(ang3)], axis=-1) / r
    c = np.arange(F_GROUP_DIM, dtype=np.float64)
    angc = 2.0 * np.pi * np.outer(c, c) / F_GROUP_DIM
    norm = 1.0 / math.sqrt(F_GROUP_DIM)
    eye = np.eye(F_GROUPS)
    wc = np.concatenate([np.kron(eye, np.cos(angc) * norm), -np.kron(eye, np.sin(angc) * norm)], axis=1)
    return w1, m3, wc


def kernel(x, c, ctx, c_ctx, w_ada, b_ada, w_in, b_in, q_norm_g, w_q_up, kv_norm_g, w_kv_up, w_fourier,
           b_fourier, w_out, b_out, post_ln_g, post_ln_b):
    b, s, _ = x.shape
    assert s == FFT_RADIX * FFT_RADIX and w_ada.shape[0] == 1
    l = 0

    wi, bi = w_in[l], b_in[l]
    o_q, o_kv, o_kr, o_gm, o_f, o_gf = 0, Q_LORA, Q_LORA + KV_LORA, Q_LORA + KV_LORA + ROPE_DIM, \
        Q_LORA + KV_LORA + ROPE_DIM + MLA_WIDTH, Q_LORA + KV_LORA + ROPE_DIM + MLA_WIDTH + F_WIDTH

    def in_cols(a):
        kr = a[..., o_kr:o_gm]
        zpad = jnp.zeros(a.shape[:-1] + (LANES - ROPE_DIM,), a.dtype)
        return jnp.concatenate([a[..., o_q:o_kr], kr, zpad, a[..., o_gm:]], axis=-1)

    w_in_p = in_cols(wi).astype(BF16)
    b_in_p = in_cols(bi)[None, :]
    w_ctx = w_in_p[:, COL_CKV:COL_GMLA]
    b_ctx = b_in_p[:, COL_CKV:COL_GMLA]

    wq = w_q_up[l].reshape(Q_LORA, MLA_HEADS, QK_DIM)
    wq_pad = jnp.pad(wq, ((0, 0), (0, 0), (0, HEAD_PAD - QK_DIM))).reshape(Q_LORA, MLA_HEADS * HEAD_PAD)
    wq_rot = _rot_cols(wq[..., NOPE_DIM:]).reshape(Q_LORA, MLA_HEADS * ROPE_DIM)
    wqt = jnp.concatenate([wq_pad, wq_rot], axis=1).T.astype(BF16)

    wkv = w_kv_up[l].reshape(KV_LORA, MLA_HEADS, NOPE_DIM + V_DIM)
    wk_pad = jnp.pad(wkv[..., :NOPE_DIM], ((0, 0), (0, 0), (0, HEAD_PAD - NOPE_DIM)))
    wk_pad = wk_pad.reshape(KV_LORA, MLA_HEADS * HEAD_PAD)
    place = np.zeros((LANES, MLA_HEADS, HEAD_PAD), np.float32)
    for j in range(ROPE_DIM):
        place[j, :, NOPE_DIM + j] = 1.0
    wk2 = jnp.concatenate([wk_pad, jnp.asarray(place.reshape(LANES, -1))], axis=0).astype(BF16)
    wvt = wkv[..., NOPE_DIM:].reshape(KV_LORA, MLA_WIDTH).T.astype(BF16)

    w1_np, m3_np, wc_np = _dft_constants()
    w1 = jnp.asarray(w1_np, F32).astype(BF16)
    m3 = jnp.asarray(m3_np, F32).astype(BF16)
    wc = jnp.asarray(wc_np, F32).astype(BF16)

    cos_np, sin_np = _rope_tables(s)
    scale = QK_SCALE
    cosq = jnp.asarray((cos_np * scale).T, F32)
    sinq = jnp.asarray((sin_np * scale).T, F32)
    kpad = np.zeros((s, LANES - ROPE_DIM))
    cosk = jnp.asarray(np.concatenate([cos_np, kpad], axis=1), F32)
    sink = jnp.asarray(np.concatenate([sin_np, kpad], axis=1), F32)

    qg = q_norm_g[l][None, :]
    kvg = kv_norm_g[l][None, :]

    rows = ((b + 1 + 7) // 8) * 8
    cvec = jnp.concatenate([c, c_ctx[None, :], jnp.zeros((rows - b - 1, D_MODEL), c.dtype)], axis=0)
    mod = _modulation(cvec, w_ada[l], b_ada[l][None, :]).reshape(rows, 3, D_MODEL)

    kc, vct = _ctx_proj(ctx, mod, b, w_ctx, b_ctx, kvg, wk2, wvt)
    qt, kl, vlt, gates, d = _lat_proj(x, mod, w_in_p, b_in_p, qg, wqt, kvg, wk2, wvt, wc,
                                      cosq, sinq, cosk, sink, tm=1024)
    z = _fft(d, w1, m3)
    attn = _attention(qt, kc, kl, vct, vlt, tq=512, sck=128)
    return _merge(x, mod, attn, z, gates, w_fourier[l].astype(BF16), b_fourier[l][None, :],
                  w_out[l].astype(BF16), b_out[l][None, :], post_ln_g[l][None, :], post_ln_b[l][None, :], tm=1024)
```

```python
import functools
import math

import jax
import jax.numpy as jnp
import numpy as np
from jax import lax
from jax.experimental import pallas as pl
from jax.experimental.pallas import tpu as pltpu

D_MODEL = 1024
CTX_LEN = 256
GRID_W = 64
MLA_HEADS = 8
NOPE_DIM = 64
ROPE_DIM = 32
V_DIM = 64
QK_DIM = NOPE_DIM + ROPE_DIM
Q_LORA = 256
KV_LORA = 128
MLA_WIDTH = MLA_HEADS * V_DIM
F_GROUPS = 4
F_GROUP_DIM = 128
F_WIDTH = F_GROUPS * F_GROUP_DIM
D_MIX = MLA_WIDTH + F_WIDTH
ROPE_BASE = 10000.0
LN_EPS = 1e-6
DEPTH = 1
DEEPNORM_ALPHA = (2.0 * DEPTH) ** 0.25
QK_SCALE = math.log2(math.e) / math.sqrt(QK_DIM)

LANES = 128
HEAD_PAD = 128
V_PAD = 80
FFT_RADIX = 64
X_PITCH = 72
Y_PITCH = 136
FFT_UNROLL = 16

COL_QLAT = 0
COL_CKV = 256
COL_KROPE = 384
COL_GMLA = 512
COL_FIN = 1024
COL_GF = 1536
D_IN_PAD = 2048

BF16 = jnp.bfloat16
F32 = jnp.float32

V7X_VMEM_BYTES = 64 * 1024 * 1024
VMEM_LIMIT = V7X_VMEM_BYTES - 8 * 1024 * 1024


def _dot(a, b):
    return jnp.dot(a, b, preferred_element_type=F32)


def _dot_nt(a, b):
    return lax.dot_general(a, b, (((1,), (1,)), ((), ())), preferred_element_type=F32)


def _layer_norm_rows(x):
    mu = jnp.mean(x, axis=-1, keepdims=True)
    xc = x - mu
    var = jnp.mean(xc * xc, axis=-1, keepdims=True)
    return xc * lax.rsqrt(var + LN_EPS)


def _rms_rows(x, g):
    return x * lax.rsqrt(jnp.mean(x * x, axis=-1, keepdims=True) + LN_EPS) * g


def _silu(x):
    return x * jax.nn.sigmoid(x)


def _mod_kernel(c_ref, w_ref, b_ref, o_ref):
    a = _silu(c_ref[...]).astype(BF16)
    o_ref[...] = _dot(a, w_ref[...].astype(BF16)) + b_ref[...]


def _modulation(cvec, w_ada, b_ada):
    rows = cvec.shape[0]
    n = w_ada.shape[1]
    tn = 1024
    return pl.pallas_call(
        _mod_kernel,
        grid=(n // tn,),
        in_specs=[pl.BlockSpec((rows, D_MODEL), lambda j: (0, 0)),
                  pl.BlockSpec((D_MODEL, tn), lambda j: (0, j)),
                  pl.BlockSpec((1, tn), lambda j: (0, j))],
        out_specs=pl.BlockSpec((rows, tn), lambda j: (0, j)),
        out_shape=jax.ShapeDtypeStruct((rows, n), F32),
        name="modulation",
    )(cvec, w_ada, b_ada)


def _store_heads(k_ref, vt_ref, k_all, vt_all):
    for hd in range(MLA_HEADS):
        k_ref[0, hd] = k_all[:, hd * HEAD_PAD:(hd + 1) * HEAD_PAD].astype(BF16)
        vt_ref[0, hd, :V_DIM, :] = vt_all[hd * V_DIM:(hd + 1) * V_DIM].astype(BF16)
        vt_ref[0, hd, V_DIM:, :] = jnp.ones((V_PAD - V_DIM, vt_all.shape[1]), BF16)


def _ctx_kernel(x_ref, mod_ref, w_ref, b_ref, kvg_ref, wk2_ref, wvt_ref, k_ref, vt_ref):
    y = _layer_norm_rows(x_ref[0])
    h = (y * (1.0 + mod_ref[0, 1:2, :]) + mod_ref[0, 0:1, :]).astype(BF16)
    proj = _dot(h, w_ref[...]) + b_ref[...]
    ckv_n = _rms_rows(proj[:, :KV_LORA], kvg_ref[...])
    ckv_b = ckv_n.astype(BF16)
    kin = jnp.concatenate([ckv_b, proj[:, KV_LORA:].astype(BF16)], axis=1)
    _store_heads(k_ref, vt_ref, _dot(kin, wk2_ref[...]), _dot_nt(wvt_ref[...], ckv_b))


def _ctx_proj(ctx, mod, mod_row, w_ctx, b_ctx, kvg, wk2, wvt):
    b, t, _ = ctx.shape
    const = lambda *shape: pl.BlockSpec(shape, lambda i: (0,) * len(shape))
    return pl.pallas_call(
        _ctx_kernel,
        grid=(b,),
        in_specs=[pl.BlockSpec((1, t, D_MODEL), lambda i: (i, 0, 0)),
                  pl.BlockSpec((1, 3, D_MODEL), lambda i: (mod_row, 0, 0)),
                  const(D_MODEL, 2 * LANES), const(1, 2 * LANES), const(1, KV_LORA),
                  const(2 * LANES, MLA_HEADS * HEAD_PAD), const(MLA_WIDTH, KV_LORA)],
        out_specs=[pl.BlockSpec((1, MLA_HEADS, t, HEAD_PAD), lambda i: (i, 0, 0, 0)),
                   pl.BlockSpec((1, MLA_HEADS, V_PAD, t), lambda i: (i, 0, 0, 0))],
        out_shape=[jax.ShapeDtypeStruct((b, MLA_HEADS, t, HEAD_PAD), BF16),
                   jax.ShapeDtypeStruct((b, MLA_HEADS, V_PAD, t), BF16)],
        compiler_params=pltpu.CompilerParams(vmem_limit_bytes=VMEM_LIMIT),
        name="ctx_proj",
    )(ctx, mod, w_ctx, b_ctx, kvg, wk2, wvt)


def _lat_kernel(x_ref, mod_ref, w_in_ref, b_in_ref, qg_ref, wqt_ref, kvg_ref, wk2_ref, wvt_ref, wc_ref,
                cosq_ref, sinq_ref, cosk_ref, sink_ref,
                qt_ref, k_ref, vt_ref, gates_ref, d_ref, h_scr):
    y = _layer_norm_rows(x_ref[0])
    h_scr[...] = (y * (1.0 + mod_ref[0, 1:2, :]) + mod_ref[0, 0:1, :]).astype(BF16)

    def proj(lo, hi):
        return _dot(h_scr[...], w_in_ref[:, lo:hi]) + b_in_ref[:, lo:hi]

    low = proj(COL_QLAT, COL_GMLA)

    qn = _rms_rows(low[:, COL_QLAT:COL_CKV], qg_ref[...]).astype(BF16)
    qt2 = _dot_nt(wqt_ref[...], qn)
    scale = QK_SCALE
    cosq = cosq_ref[...]
    sinq = sinq_ref[...]
    rot_base = MLA_HEADS * HEAD_PAD
    for hd in range(MLA_HEADS):
        base = hd * HEAD_PAD
        qt_ref[0, hd, :NOPE_DIM, :] = (qt2[base:base + NOPE_DIM] * scale).astype(BF16)
        rope = (qt2[base + NOPE_DIM:base + QK_DIM] * cosq
                + qt2[rot_base + hd * ROPE_DIM:rot_base + (hd + 1) * ROPE_DIM] * sinq)
        qt_ref[0, hd, NOPE_DIM:QK_DIM, :] = rope.astype(BF16)
        qt_ref[0, hd, QK_DIM:, :] = jnp.zeros((HEAD_PAD - QK_DIM, qt2.shape[1]), BF16)

    ckv_b = _rms_rows(low[:, COL_CKV:COL_KROPE], kvg_ref[...]).astype(BF16)
    kr = low[:, COL_KROPE:COL_GMLA]
    half = ROPE_DIM // 2
    rot = pltpu.roll(kr, half, axis=1) - pltpu.roll(kr, LANES - half, axis=1)
    kro = kr * cosk_ref[...] + rot * sink_ref[...]
    kin = jnp.concatenate([ckv_b, kro.astype(BF16)], axis=1)
    _store_heads(k_ref, vt_ref, _dot(kin, wk2_ref[...]), _dot_nt(wvt_ref[...], ckv_b))

    gates_ref[0, :, :MLA_WIDTH] = _silu(proj(COL_GMLA, COL_FIN)).astype(BF16)
    gates_ref[0, :, MLA_WIDTH:] = _silu(proj(COL_GF, D_IN_PAD)).astype(BF16)

    d_ref[0] = _dot(proj(COL_FIN, COL_GF).astype(BF16), wc_ref[...]).astype(BF16)


def _lat_proj(x, mod, w_in_p, b_in_p, qg, wqt, kvg, wk2, wvt, wc, cosq, sinq, cosk, sink, tm):
    b, s, _ = x.shape
    const = lambda *shape: pl.BlockSpec(shape, lambda i, j: (0,) * len(shape))
    hp = MLA_HEADS * HEAD_PAD
    return pl.pallas_call(
        _lat_kernel,
        grid=(b, s // tm),
        in_specs=[pl.BlockSpec((1, tm, D_MODEL), lambda i, j: (i, j, 0)),
                  pl.BlockSpec((1, 3, D_MODEL), lambda i, j: (i, 0, 0)),
                  const(D_MODEL, D_IN_PAD), const(1, D_IN_PAD), const(1, Q_LORA),
                  const(hp + MLA_HEADS * ROPE_DIM, Q_LORA), const(1, KV_LORA),
                  const(2 * LANES, hp), const(MLA_WIDTH, KV_LORA), const(F_WIDTH, 2 * F_WIDTH),
                  pl.BlockSpec((ROPE_DIM, tm), lambda i, j: (0, j)),
                  pl.BlockSpec((ROPE_DIM, tm), lambda i, j: (0, j)),
                  pl.BlockSpec((tm, LANES), lambda i, j: (j, 0)),
                  pl.BlockSpec((tm, LANES), lambda i, j: (j, 0))],
        out_specs=[pl.BlockSpec((1, MLA_HEADS, HEAD_PAD, tm), lambda i, j: (i, 0, 0, j)),
                   pl.BlockSpec((1, MLA_HEADS, tm, HEAD_PAD), lambda i, j: (i, 0, j, 0)),
                   pl.BlockSpec((1, MLA_HEADS, V_PAD, tm), lambda i, j: (i, 0, 0, j)),
                   pl.BlockSpec((1, tm, D_MIX), lambda i, j: (i, j, 0)),
                   pl.BlockSpec((1, tm, 2 * F_WIDTH), lambda i, j: (i, j, 0))],
        out_shape=[jax.ShapeDtypeStruct((b, MLA_HEADS, HEAD_PAD, s), BF16),
                   jax.ShapeDtypeStruct((b, MLA_HEADS, s, HEAD_PAD), BF16),
                   jax.ShapeDtypeStruct((b, MLA_HEADS, V_PAD, s), BF16),
                   jax.ShapeDtypeStruct((b, s, D_MIX), BF16),
                   jax.ShapeDtypeStruct((b, s, 2 * F_WIDTH), BF16)],
        scratch_shapes=[pltpu.VMEM((tm, D_MODEL), BF16)],
        compiler_params=pltpu.CompilerParams(vmem_limit_bytes=VMEM_LIMIT),
        name="lat_proj",
    )(x, mod, w_in_p, b_in_p, qg, wqt, kvg, wk2, wvt, wc, cosq, sinq, cosk, sink)


def _fft_kernel(dr_ref, di_ref, w1_ref, m3_ref, z_ref, xs, ys, zs):
    r = FFT_RADIX

    def fill(t1, carry):
        src = pl.multiple_of(t1 * r, r)
        dst = pl.multiple_of(t1 * X_PITCH, 8)
        re = dr_ref[0, pl.ds(src, r), :].astype(F32)
        im = di_ref[0, pl.ds(src, r), :].astype(F32)
        xs[0, pl.ds(dst, r), :] = re[:, :LANES]
        xs[1, pl.ds(dst, r), :] = re[:, LANES:]
        xs[2, pl.ds(dst, r), :] = im[:, :LANES]
        xs[3, pl.ds(dst, r), :] = im[:, LANES:]
        return carry

    lax.fori_loop(0, r, fill, 0)

    def stage1(g, carry):
        cols = []
        for u in range(FFT_UNROLL):
            t2 = g * FFT_UNROLL + u
            parts = [xs[sl, pl.ds(t2, r, stride=X_PITCH), :] for sl in range(4)]
            cols.append(jnp.concatenate([jnp.concatenate(parts[0:2], axis=1),
                                         jnp.concatenate(parts[2:4], axis=1)], axis=0))
        rhs = jnp.concatenate(cols, axis=1).astype(BF16)
        y = _dot(w1_ref[...], rhs)
        for u in range(FFT_UNROLL):
            dst = pl.multiple_of((g * FFT_UNROLL + u) * Y_PITCH, 8)
            ys[0, pl.ds(dst, 2 * r), :] = y[:, (2 * u) * LANES:(2 * u + 1) * LANES]
            ys[1, pl.ds(dst, 2 * r), :] = y[:, (2 * u + 1) * LANES:(2 * u + 2) * LANES]
        return carry

    lax.fori_loop(0, r // FFT_UNROLL, stage1, 0)

    def stage2(g, carry):
        for u in range(FFT_UNROLL):
            k1 = g * FFT_UNROLL + u
            yr = jnp.concatenate([ys[sl, pl.ds(k1, r, stride=Y_PITCH), :] for sl in range(2)], axis=1)
            yi = jnp.concatenate([ys[sl, pl.ds(r + k1, r, stride=Y_PITCH), :] for sl in range(2)], axis=1)
            rhs = jnp.concatenate([yr, yi], axis=0).astype(BF16)
            z = _dot(m3_ref[k1], rhs)
            zs[0, pl.ds(k1, r, stride=X_PITCH), :] = z[:, :LANES]
            zs[1, pl.ds(k1, r, stride=X_PITCH), :] = z[:, LANES:]
        return carry

    lax.fori_loop(0, r // FFT_UNROLL, stage2, 0)

    def drain(k2, carry):
        src = pl.multiple_of(k2 * X_PITCH, 8)
        dst = pl.multiple_of(k2 * r, r)
        z_ref[0, pl.ds(dst, r), :] = jnp.concatenate(
            [zs[0, pl.ds(src, r), :], zs[1, pl.ds(src, r), :]], axis=1).astype(BF16)
        return carry

    lax.fori_loop(0, r, drain, 0)


def _fft(d, w1, m3):
    b, s, _ = d.shape
    r = FFT_RADIX
    cw = 2 * LANES
    nblk = F_WIDTH // cw
    return pl.pallas_call(
        _fft_kernel,
        grid=(b, nblk),
        in_specs=[pl.BlockSpec((1, s, cw), lambda i, g: (i, 0, g)),
                  pl.BlockSpec((1, s, cw), lambda i, g: (i, 0, nblk + g)),
                  pl.BlockSpec((2 * r, 2 * r), lambda i, g: (0, 0)),
                  pl.BlockSpec((r, r, 2 * r), lambda i, g: (0, 0, 0))],
        out_specs=pl.BlockSpec((1, s, cw), lambda i, g: (i, 0, g)),
        out_shape=jax.ShapeDtypeStruct((b, s, F_WIDTH), BF16),
        scratch_shapes=[pltpu.VMEM((4, r * X_PITCH, LANES), F32),
                        pltpu.VMEM((2, r * Y_PITCH, LANES), F32),
                        pltpu.VMEM((2, r * X_PITCH, LANES), F32)],
        compiler_params=pltpu.CompilerParams(vmem_limit_bytes=VMEM_LIMIT),
        name="fft",
    )(d, d, w1, m3)


def _attn_kernel(qt_ref, kc_ref, kl_ref, vct_ref, vlt_ref, o_ref,
                 s_scr0, s_scr1, p_scr0, p_scr1, acc_scr, m_scr, o_scr, *, nq, sck):
    j = pl.program_id(1)
    tq = qt_ref.shape[3]
    n_ctx = kc_ref.shape[2]
    n_lat = kl_ref.shape[2]
    nh = MLA_HEADS
    n_body = 2
    lat_rows = n_lat // n_body
    ctx_rows = n_ctx // n_body

    def fold8(v, op):
        return op(v.reshape(v.shape[0] // 8, 8, tq), axis=0)

    s_scr = (s_scr0, s_scr1)
    p_scr = (p_scr0, p_scr1)

    def phase(h, s_qk, blk, do_qk, do_ex, do_pv):
        h_pv = (h + nh - 2) % nh
        s_ex = 1 - s_qk
        if do_ex:
            mb = jnp.broadcast_to(jnp.max(m_scr[s_ex], axis=0, keepdims=True), (8, tq))
        if do_pv:
            acc_scr[...] = jnp.zeros_like(acc_scr)

        def work(k_ref, src, row0, n, m8c):
            rows = pl.ds(row0, n)
            if do_ex:
                sv = s_scr[s_ex][rows, :]
                p = jnp.exp2(sv.reshape(n // 8, 8, tq) - mb[None]).reshape(n, tq)
                p_scr[s_ex][rows, :] = p.astype(BF16)
            if do_qk:
                s = _dot(k_ref[0, h, pl.ds(src, n), :], qt_ref[0, h])
                s_scr[s_qk][rows, :] = s
                m8c = jnp.maximum(m8c, fold8(s, jnp.max))
            return m8c

        def body(c, m8c):
            lat0 = pl.multiple_of(c * lat_rows, lat_rows)
            ctx0 = pl.multiple_of(c * ctx_rows, ctx_rows)
            for sub in range(lat_rows // sck):
                m8c = work(kl_ref, lat0 + sub * sck, lat0 + sub * sck, sck, m8c)
            m8c = work(kc_ref, ctx0, n_lat + ctx0, ctx_rows, m8c)
            if do_pv:
                acc_scr[...] += (
                    _dot(vlt_ref[0, h_pv, :, pl.ds(lat0, lat_rows)], p_scr[s_qk][pl.ds(lat0, lat_rows), :])
                    + _dot(vct_ref[0, h_pv, :, pl.ds(ctx0, ctx_rows)], p_scr[s_qk][pl.ds(n_lat + ctx0, ctx_rows), :]))
            return m8c

        m8 = lax.fori_loop(0, n_body, body, jnp.full((8, tq), -jnp.inf, F32))
        if do_pv:
            pv_blk = jnp.where(jnp.asarray(h) < 2, blk - 1, blk)
            acc = acc_scr[...]
            o_scr[pv_blk % 2, h_pv] = acc[:V_DIM] / acc[V_DIM:V_DIM + 1]
        if do_qk:
            m_scr[s_qk] = m8

    @pl.when(j == 0)
    def _():
        phase(0, 0, 0, True, False, False)
        phase(1, 1, 0, True, True, False)

    @pl.when(j < nq)
    def _():
        def run(pair, carry):
            phase(2 * pair, 0, j, True, True, True)
            phase(2 * pair + 1, 1, j, True, True, True)
            return carry
        lax.fori_loop(jnp.where(j == 0, 1, 0), nh // 2, run, 0)

    @pl.when(j == nq)
    def _():
        phase(0, 0, nq, False, True, True)
        phase(1, 1, nq, False, False, True)

    @pl.when(j > 0)
    def _():
        done = o_scr[(j - 1) % 2]
        o_ref[0] = done.reshape(nh * V_DIM, tq).T.astype(BF16)


def _attention(qt, kc, kl, vct, vlt, tq, sck):
    b, nh, hp, s = qt.shape
    n_ctx = kc.shape[2]
    nq = s // tq
    once = dict(pipeline_mode=pl.Buffered(1))
    k_map = lambda i, j: (jnp.minimum(i + (j == nq).astype(jnp.int32), b - 1), 0, 0, 0)
    return pl.pallas_call(
        functools.partial(_attn_kernel, nq=nq, sck=sck),
        grid=(b, nq + 1),
        in_specs=[pl.BlockSpec((1, nh, hp, tq), lambda i, j: (i, 0, 0, jnp.minimum(j, nq - 1))),
                  pl.BlockSpec((1, nh, n_ctx, hp), k_map),
                  pl.BlockSpec((1, nh, s, hp), k_map),
                  pl.BlockSpec((1, nh, V_PAD, n_ctx), lambda i, j: (i, 0, 0, 0), **once),
                  pl.BlockSpec((1, nh, V_PAD, s), lambda i, j: (i, 0, 0, 0), **once)],
        out_specs=pl.BlockSpec((1, tq, MLA_WIDTH), lambda i, j: (i, jnp.maximum(j - 1, 0), 0)),
        out_shape=jax.ShapeDtypeStruct((b, s, MLA_WIDTH), BF16),
        scratch_shapes=[pltpu.VMEM((n_ctx + s, tq), F32), pltpu.VMEM((n_ctx + s, tq), F32),
                        pltpu.VMEM((n_ctx + s, tq), BF16), pltpu.VMEM((n_ctx + s, tq), BF16),
                        pltpu.VMEM((V_PAD, tq), F32),
                        pltpu.VMEM((2, 8, tq), F32),
                        pltpu.VMEM((2, nh, V_DIM, tq), F32)],
        compiler_params=pltpu.CompilerParams(vmem_limit_bytes=VMEM_LIMIT,
                                             dimension_semantics=("arbitrary", "arbitrary")),
        name="attention",
    )(qt, kc, kl, vct, vlt)


def _merge_kernel(x_ref, mod_ref, attn_ref, z_ref, gates_ref, wf_ref, bf_ref, wo_ref, bo_ref, g_ref, b_ref, o_ref):
    four = _dot(z_ref[0], wf_ref[...]) + bf_ref[...]
    gates = gates_ref[0]
    ya = attn_ref[0].astype(F32) * gates[:, :MLA_WIDTH].astype(F32)
    yf = four * gates[:, MLA_WIDTH:].astype(F32)
    ycat = jnp.concatenate([ya.astype(BF16), yf.astype(BF16)], axis=1)
    y = _dot(ycat, wo_ref[...]) + bo_ref[...]
    r = DEEPNORM_ALPHA * x_ref[0] + mod_ref[0, 2:3, :] * y
    o_ref[0] = _layer_norm_rows(r) * g_ref[...] + b_ref[...]


def _merge(x, mod, attn, z, gates, wf, bf, wo, bo, g, bb, tm):
    b, s, _ = x.shape
    const = lambda *shape: pl.BlockSpec(shape, lambda i, j: (0,) * len(shape))
    tok = lambda w: pl.BlockSpec((1, tm, w), lambda i, j: (i, j, 0))
    return pl.pallas_call(
        _merge_kernel,
        grid=(b, s // tm),
        in_specs=[tok(D_MODEL), pl.BlockSpec((1, 3, D_MODEL), lambda i, j: (i, 0, 0)),
                  tok(MLA_WIDTH), tok(F_WIDTH), tok(D_MIX),
                  const(F_WIDTH, F_WIDTH), const(1, F_WIDTH), const(D_MIX, D_MODEL), const(1, D_MODEL),
                  const(1, D_MODEL), const(1, D_MODEL)],
        out_specs=tok(D_MODEL),
        out_shape=jax.ShapeDtypeStruct((b, s, D_MODEL), x.dtype),
        compiler_params=pltpu.CompilerParams(vmem_limit_bytes=VMEM_LIMIT),
        name="merge",
    )(x, mod, attn, z, gates, wf, bf, wo, bo, g, bb)


def _rot_cols(w):
    half = w.shape[-1] // 2
    return jnp.concatenate([-w[..., half:], w[..., :half]], axis=-1)


def _rope_tables(s):
    n_rows = s // GRID_W
    pos = np.arange(s)
    rows = (pos // GRID_W).astype(np.float64)
    cols = (pos % GRID_W).astype(np.float64)
    assert n_rows * GRID_W == s
    axis_dim = ROPE_DIM // 2
    inv_freq = ROPE_BASE ** (-np.arange(0, axis_dim, 2, dtype=np.float64) / axis_dim)
    ang = np.concatenate([rows[:, None] * inv_freq, cols[:, None] * inv_freq], axis=-1)
    ang = np.concatenate([ang, ang], axis=-1)
    return np.cos(ang), np.sin(ang)


def _dft_constants():
    r = FFT_RADIX
    n = r * r
    k = np.arange(r, dtype=np.float64)
    ang1 = 2.0 * np.pi * np.outer(k, k) / r
    c1, s1 = np.cos(ang1), np.sin(ang1)
    w1 = np.block([[c1, s1], [-s1, c1]])
    k1 = np.arange(r)[:, None, None]
    k2 = np.arange(r)[None, :, None]
    t2 = np.arange(r)[None, None, :]
    ang3 = 2.0 * np.pi * t2 * (k1 + r * k2) / n
    m3 = np.concatenate([np.cos(ang3), np.sin(ang3)], axis=-1) / r
    c = np.arange(F_GROUP_DIM, dtype=np.float64)
    angc = 2.0 * np.pi * np.outer(c, c) / F_GROUP_DIM
    norm = 1.0 / math.sqrt(F_GROUP_DIM)
    eye = np.eye(F_GROUPS)
    wc = np.concatenate([np.kron(eye, np.cos(angc) * norm), -np.kron(eye, np.sin(angc) * norm)], axis=1)
    return w1, m3, wc


def kernel(x, c, ctx, c_ctx, w_ada, b_ada, w_in, b_in, q_norm_g, w_q_up, kv_norm_g, w_kv_up, w_fourier,
           b_fourier, w_out, b_out, post_ln_g, post_ln_b):
    b, s, _ = x.shape
    assert s == FFT_RADIX * FFT_RADIX and w_ada.shape[0] == 1
    l = 0

    wi, bi = w_in[l], b_in[l]
    o_q, o_kv, o_kr, o_gm, o_f, o_gf = 0, Q_LORA, Q_LORA + KV_LORA, Q_LORA + KV_LORA + ROPE_DIM, \
        Q_LORA + KV_LORA + ROPE_DIM + MLA_WIDTH, Q_LORA + KV_LORA + ROPE_DIM + MLA_WIDTH + F_WIDTH

    def in_cols(a):
        kr = a[..., o_kr:o_gm]
        zpad = jnp.zeros(a.shape[:-1] + (LANES - ROPE_DIM,), a.dtype)
        return jnp.concatenate([a[..., o_q:o_kr], kr, zpad, a[..., o_gm:]], axis=-1)

    w_in_p = in_cols(wi).astype(BF16)
    b_in_p = in_cols(bi)[None, :]
    w_ctx = w_in_p[:, COL_CKV:COL_GMLA]
    b_ctx = b_in_p[:, COL_CKV:COL_GMLA]

    wq = w_q_up[l].reshape(Q_LORA, MLA_HEADS, QK_DIM)
    wq_pad = jnp.pad(wq, ((0, 0), (0, 0), (0, HEAD_PAD - QK_DIM))).reshape(Q_LORA, MLA_HEADS * HEAD_PAD)
    wq_rot = _rot_cols(wq[..., NOPE_DIM:]).reshape(Q_LORA, MLA_HEADS * ROPE_DIM)
    wqt = jnp.concatenate([wq_pad, wq_rot], axis=1).T.astype(BF16)

    wkv = w_kv_up[l].reshape(KV_LORA, MLA_HEADS, NOPE_DIM + V_DIM)
    wk_pad = jnp.pad(wkv[..., :NOPE_DIM], ((0, 0), (0, 0), (0, HEAD_PAD - NOPE_DIM)))
    wk_pad = wk_pad.reshape(KV_LORA, MLA_HEADS * HEAD_PAD)
    place = np.zeros((LANES, MLA_HEADS, HEAD_PAD), np.float32)
    for j in range(ROPE_DIM):
        place[j, :, NOPE_DIM + j] = 1.0
    wk2 = jnp.concatenate([wk_pad, jnp.asarray(place.reshape(LANES, -1))], axis=0).astype(BF16)
    wvt = wkv[..., NOPE_DIM:].reshape(KV_LORA, MLA_WIDTH).T.astype(BF16)

    w1_np, m3_np, wc_np = _dft_constants()
    w1 = jnp.asarray(w1_np, F32).astype(BF16)
    m3 = jnp.asarray(m3_np, F32).astype(BF16)
    wc = jnp.asarray(wc_np, F32).astype(BF16)

    cos_np, sin_np = _rope_tables(s)
    scale = QK_SCALE
    cosq = jnp.asarray((cos_np * scale).T, F32)
    sinq = jnp.asarray((sin_np * scale).T, F32)
    kpad = np.zeros((s, LANES - ROPE_DIM))
    cosk = jnp.asarray(np.concatenate([cos_np, kpad], axis=1), F32)
    sink = jnp.asarray(np.concatenate([sin_np, kpad], axis=1), F32)

    qg = q_norm_g[l][None, :]
    kvg = kv_norm_g[l][None, :]

    rows = ((b + 1 + 7) // 8) * 8
    cvec = jnp.concatenate([c, c_ctx[None, :], jnp.zeros((rows - b - 1, D_MODEL), c.dtype)], axis=0)
    mod = _modulation(cvec, w_ada[l], b_ada[l][None, :]).reshape(rows, 3, D_MODEL)

    kc, vct = _ctx_proj(ctx, mod, b, w_ctx, b_ctx, kvg, wk2, wvt)
    qt, kl, vlt, gates, d = _lat_proj(x, mod, w_in_p, b_in_p, qg, wqt, kvg, wk2, wvt, wc,
                                      cosq, sinq, cosk, sink, tm=1024)
    z = _fft(d, w1, m3)
    attn = _attention(qt, kc, kl, vct, vlt, tq=512, sck=128)
    return _merge(x, mod, attn, z, gates, w_fourier[l].astype(BF16), b_fourier[l][None, :],
                  w_out[l].astype(BF16), b_out[l][None, :], post_ln_g[l][None, :], post_ln_b[l][None, :], tm=1024)
```

```python
import functools
import math

import jax
import jax.numpy as jnp
import numpy as np
from jax import lax
from jax.experimental import pallas as pl
from jax.experimental.pallas import tpu as pltpu

D_MODEL = 1024
GRID_W = 64
MLA_HEADS = 8
NOPE_DIM = 64
ROPE_DIM = 32
V_DIM = 64
QK_DIM = NOPE_DIM + ROPE_DIM
Q_LORA = 256
KV_LORA = 128
MLA_WIDTH = MLA_HEADS * V_DIM
F_GROUPS = 4
F_GROUP_DIM = 128
F_WIDTH = F_GROUPS * F_GROUP_DIM
D_MIX = MLA_WIDTH + F_WIDTH
ROPE_BASE = 10000.0
LN_EPS = 1e-6
DEPTH = 1
DEEPNORM_ALPHA = (2.0 * DEPTH) ** 0.25
QK_SCALE = math.log2(math.e) / math.sqrt(QK_DIM)

LANES = 128
HEAD_PAD = 128
V_PAD = 80
FFT_RADIX = 64
X_PITCH = 72
Y_PITCH = 136
FFT_UNROLL = 16

COL_QLAT = 0
COL_CKV = 256
COL_KROPE = 384
COL_GMLA = 512
COL_FIN = 1024
COL_GF = 1536
D_IN_PAD = 2048

BF16 = jnp.bfloat16
F32 = jnp.float32

V7X_VMEM_BYTES = 64 * 1024 * 1024
VMEM_LIMIT = V7X_VMEM_BYTES - 8 * 1024 * 1024

PROJ_TOKENS = 1024
ATTN_QUERIES = 512
ATTN_SUB_KEYS = 128
ATTN_BODIES = 2
MOD_COLS = 1024
FFT_COPY_UNROLL = 4


def _dot(a, b):
    return jnp.dot(a, b, preferred_element_type=F32)


def _dot_nt(a, b):
    return lax.dot_general(a, b, (((1,), (1,)), ((), ())), preferred_element_type=F32)


def _layer_norm_rows(x):
    mu = jnp.mean(x, axis=-1, keepdims=True)
    xc = x - mu
    var = jnp.mean(xc * xc, axis=-1, keepdims=True)
    return xc * lax.rsqrt(var + LN_EPS)


def _rms_rows(x, g):
    return x * lax.rsqrt(jnp.mean(x * x, axis=-1, keepdims=True) + LN_EPS) * g


def _silu(x):
    return x * jax.nn.sigmoid(x)


def _mod_kernel(c_ref, w_ref, b_ref, o_ref):
    a = _silu(c_ref[...]).astype(BF16)
    o_ref[...] = _dot(a, w_ref[...].astype(BF16)) + b_ref[...]


def _modulation(cvec, w_ada, b_ada):
    rows = cvec.shape[0]
    n = w_ada.shape[1]
    tn = MOD_COLS
    return pl.pallas_call(
        _mod_kernel,
        grid=(n // tn,),
        in_specs=[pl.BlockSpec((rows, D_MODEL), lambda j: (0, 0)),
                  pl.BlockSpec((D_MODEL, tn), lambda j: (0, j)),
                  pl.BlockSpec((1, tn), lambda j: (0, j))],
        out_specs=pl.BlockSpec((rows, tn), lambda j: (0, j)),
        out_shape=jax.ShapeDtypeStruct((rows, n), F32),
        name="modulation",
    )(cvec, w_ada, b_ada)


def _store_heads(k_ref, vt_ref, k_all, vt_all):
    for hd in range(MLA_HEADS):
        k_ref[0, hd] = k_all[:, hd * HEAD_PAD:(hd + 1) * HEAD_PAD].astype(BF16)
        vt_ref[0, hd, :V_DIM, :] = vt_all[hd * V_DIM:(hd + 1) * V_DIM].astype(BF16)
        vt_ref[0, hd, V_DIM:, :] = jnp.ones((V_PAD - V_DIM, vt_all.shape[1]), BF16)


def _ctx_kernel(x_ref, mod_ref, w_ref, b_ref, kvg_ref, wk2_ref, wvt_ref, k_ref, vt_ref):
    y = _layer_norm_rows(x_ref[0])
    h = (y * (1.0 + mod_ref[0, 1:2, :]) + mod_ref[0, 0:1, :]).astype(BF16)
    proj = _dot(h, w_ref[...]) + b_ref[...]
    ckv_n = _rms_rows(proj[:, :KV_LORA], kvg_ref[...])
    ckv_b = ckv_n.astype(BF16)
    kin = jnp.concatenate([ckv_b, proj[:, KV_LORA:].astype(BF16)], axis=1)
    _store_heads(k_ref, vt_ref, _dot(kin, wk2_ref[...]), _dot_nt(wvt_ref[...], ckv_b))


def _ctx_proj(ctx, mod, mod_row, w_ctx, b_ctx, kvg, wk2, wvt):
    b, t, _ = ctx.shape
    const = lambda *shape: pl.BlockSpec(shape, lambda i: (0,) * len(shape))
    return pl.pallas_call(
        _ctx_kernel,
        grid=(b,),
        in_specs=[pl.BlockSpec((1, t, D_MODEL), lambda i: (i, 0, 0)),
                  pl.BlockSpec((1, 3, D_MODEL), lambda i: (mod_row, 0, 0)),
                  const(D_MODEL, 2 * LANES), const(1, 2 * LANES), const(1, KV_LORA),
                  const(2 * LANES, MLA_HEADS * HEAD_PAD), const(MLA_WIDTH, KV_LORA)],
        out_specs=[pl.BlockSpec((1, MLA_HEADS, t, HEAD_PAD), lambda i: (i, 0, 0, 0)),
                   pl.BlockSpec((1, MLA_HEADS, V_PAD, t), lambda i: (i, 0, 0, 0))],
        out_shape=[jax.ShapeDtypeStruct((b, MLA_HEADS, t, HEAD_PAD), BF16),
                   jax.ShapeDtypeStruct((b, MLA_HEADS, V_PAD, t), BF16)],
        compiler_params=pltpu.CompilerParams(vmem_limit_bytes=VMEM_LIMIT),
        name="ctx_proj",
    )(ctx, mod, w_ctx, b_ctx, kvg, wk2, wvt)


def _lat_kernel(x_ref, mod_ref, w_in_ref, b_in_ref, qg_ref, wqt_ref, kvg_ref, wk2_ref, wvt_ref, wc_ref,
                cosq_ref, sinq_ref, cosk_ref, sink_ref,
                qt_ref, k_ref, vt_ref, gates_ref, d_ref, h_scr):
    y = _layer_norm_rows(x_ref[0])
    h_scr[...] = (y * (1.0 + mod_ref[0, 1:2, :]) + mod_ref[0, 0:1, :]).astype(BF16)

    def proj(lo, hi):
        return _dot(h_scr[...], w_in_ref[:, lo:hi]) + b_in_ref[:, lo:hi]

    low = proj(COL_QLAT, COL_GMLA)

    qn = _rms_rows(low[:, COL_QLAT:COL_CKV], qg_ref[...]).astype(BF16)
    qt2 = _dot_nt(wqt_ref[...], qn)
    scale = QK_SCALE
    cosq = cosq_ref[...]
    sinq = sinq_ref[...]
    rot_base = MLA_HEADS * HEAD_PAD
    for hd in range(MLA_HEADS):
        base = hd * HEAD_PAD
        qt_ref[0, hd, :NOPE_DIM, :] = (qt2[base:base + NOPE_DIM] * scale).astype(BF16)
        rope = (qt2[base + NOPE_DIM:base + QK_DIM] * cosq
                + qt2[rot_base + hd * ROPE_DIM:rot_base + (hd + 1) * ROPE_DIM] * sinq)
        qt_ref[0, hd, NOPE_DIM:QK_DIM, :] = rope.astype(BF16)
        qt_ref[0, hd, QK_DIM:, :] = jnp.zeros((HEAD_PAD - QK_DIM, qt2.shape[1]), BF16)

    ckv_b = _rms_rows(low[:, COL_CKV:COL_KROPE], kvg_ref[...]).astype(BF16)
    kr = low[:, COL_KROPE:COL_GMLA]
    half = ROPE_DIM // 2
    rot = pltpu.roll(kr, half, axis=1) - pltpu.roll(kr, LANES - half, axis=1)
    kro = kr * cosk_ref[...] + rot * sink_ref[...]
    kin = jnp.concatenate([ckv_b, kro.astype(BF16)], axis=1)
    _store_heads(k_ref, vt_ref, _dot(kin, wk2_ref[...]), _dot_nt(wvt_ref[...], ckv_b))

    gates_ref[0, :, :MLA_WIDTH] = _silu(proj(COL_GMLA, COL_FIN)).astype(BF16)
    gates_ref[0, :, MLA_WIDTH:] = _silu(proj(COL_GF, D_IN_PAD)).astype(BF16)

    d_ref[0] = _dot(proj(COL_FIN, COL_GF).astype(BF16), wc_ref[...]).astype(BF16)


def _lat_proj(x, mod, w_in_p, b_in_p, qg, wqt, kvg, wk2, wvt, wc, cosq, sinq, cosk, sink, tm):
    b, s, _ = x.shape
    const = lambda *shape: pl.BlockSpec(shape, lambda i, j: (0,) * len(shape))
    hp = MLA_HEADS * HEAD_PAD
    return pl.pallas_call(
        _lat_kernel,
        grid=(b, s // tm),
        in_specs=[pl.BlockSpec((1, tm, D_MODEL), lambda i, j: (i, j, 0)),
                  pl.BlockSpec((1, 3, D_MODEL), lambda i, j: (i, 0, 0)),
                  const(D_MODEL, D_IN_PAD), const(1, D_IN_PAD), const(1, Q_LORA),
                  const(hp + MLA_HEADS * ROPE_DIM, Q_LORA), const(1, KV_LORA),
                  const(2 * LANES, hp), const(MLA_WIDTH, KV_LORA), const(F_WIDTH, 2 * F_WIDTH),
                  pl.BlockSpec((ROPE_DIM, tm), lambda i, j: (0, j)),
                  pl.BlockSpec((ROPE_DIM, tm), lambda i, j: (0, j)),
                  pl.BlockSpec((tm, LANES), lambda i, j: (j, 0)),
                  pl.BlockSpec((tm, LANES), lambda i, j: (j, 0))],
        out_specs=[pl.BlockSpec((1, MLA_HEADS, HEAD_PAD, tm), lambda i, j: (i, 0, 0, j)),
                   pl.BlockSpec((1, MLA_HEADS, tm, HEAD_PAD), lambda i, j: (i, 0, j, 0)),
                   pl.BlockSpec((1, MLA_HEADS, V_PAD, tm), lambda i, j: (i, 0, 0, j)),
                   pl.BlockSpec((1, tm, D_MIX), lambda i, j: (i, j, 0)),
                   pl.BlockSpec((1, tm, 2 * F_WIDTH), lambda i, j: (i, j, 0))],
        out_shape=[jax.ShapeDtypeStruct((b, MLA_HEADS, HEAD_PAD, s), BF16),
                   jax.ShapeDtypeStruct((b, MLA_HEADS, s, HEAD_PAD), BF16),
                   jax.ShapeDtypeStruct((b, MLA_HEADS, V_PAD, s), BF16),
                   jax.ShapeDtypeStruct((b, s, D_MIX), BF16),
                   jax.ShapeDtypeStruct((b, s, 2 * F_WIDTH), BF16)],
        scratch_shapes=[pltpu.VMEM((tm, D_MODEL), BF16)],
        compiler_params=pltpu.CompilerParams(vmem_limit_bytes=VMEM_LIMIT),
        name="lat_proj",
    )(x, mod, w_in_p, b_in_p, qg, wqt, kvg, wk2, wvt, wc, cosq, sinq, cosk, sink)


def _fft_kernel(dr_ref, di_ref, w1_ref, m3_ref, z_ref, xs, ys, zs):
    r = FFT_RADIX

    def fill(g, carry):
        for u in range(FFT_COPY_UNROLL):
            t1 = g * FFT_COPY_UNROLL + u
            src = pl.multiple_of(t1 * r, r)
            dst = pl.multiple_of(t1 * X_PITCH, 8)
            re = dr_ref[0, pl.ds(src, r), :].astype(F32)
            im = di_ref[0, pl.ds(src, r), :].astype(F32)
            xs[0, pl.ds(dst, r), :] = re[:, :LANES]
            xs[1, pl.ds(dst, r), :] = re[:, LANES:]
            xs[2, pl.ds(dst, r), :] = im[:, :LANES]
            xs[3, pl.ds(dst, r), :] = im[:, LANES:]
        return carry

    lax.fori_loop(0, r // FFT_COPY_UNROLL, fill, 0)

    def stage1(g, carry):
        cols = []
        for u in range(FFT_UNROLL):
            t2 = g * FFT_UNROLL + u
            parts = [xs[sl, pl.ds(t2, r, stride=X_PITCH), :] for sl in range(4)]
            cols.append(jnp.concatenate([jnp.concatenate(parts[0:2], axis=1),
                                         jnp.concatenate(parts[2:4], axis=1)], axis=0))
        rhs = jnp.concatenate(cols, axis=1).astype(BF16)
        y = _dot(w1_ref[...], rhs)
        for u in range(FFT_UNROLL):
            dst = pl.multiple_of((g * FFT_UNROLL + u) * Y_PITCH, 8)
            ys[0, pl.ds(dst, 2 * r), :] = y[:, (2 * u) * LANES:(2 * u + 1) * LANES]
            ys[1, pl.ds(dst, 2 * r), :] = y[:, (2 * u + 1) * LANES:(2 * u + 2) * LANES]
        return carry

    lax.fori_loop(0, r // FFT_UNROLL, stage1, 0)

    def stage2(g, carry):
        for u in range(FFT_UNROLL):
            k1 = g * FFT_UNROLL + u
            yr = jnp.concatenate([ys[sl, pl.ds(k1, r, stride=Y_PITCH), :] for sl in range(2)], axis=1)
            yi = jnp.concatenate([ys[sl, pl.ds(r + k1, r, stride=Y_PITCH), :] for sl in range(2)], axis=1)
            rhs = jnp.concatenate([yr, yi], axis=0).astype(BF16)
            z = _dot(m3_ref[k1], rhs)
            zs[0, pl.ds(k1, r, stride=X_PITCH), :] = z[:, :LANES]
            zs[1, pl.ds(k1, r, stride=X_PITCH), :] = z[:, LANES:]
        return carry

    lax.fori_loop(0, r // FFT_UNROLL, stage2, 0)

    def drain(g, carry):
        for u in range(FFT_COPY_UNROLL):
            k2 = g * FFT_COPY_UNROLL + u
            src = pl.multiple_of(k2 * X_PITCH, 8)
            dst = pl.multiple_of(k2 * r, r)
            z_ref[0, pl.ds(dst, r), :] = jnp.concatenate(
                [zs[0, pl.ds(src, r), :], zs[1, pl.ds(src, r), :]], axis=1).astype(BF16)
        return carry

    lax.fori_loop(0, r // FFT_COPY_UNROLL, drain, 0)


def _fft(d, w1, m3):
    b, s, _ = d.shape
    r = FFT_RADIX
    cw = 2 * LANES
    nblk = F_WIDTH // cw
    return pl.pallas_call(
        _fft_kernel,
        grid=(b, nblk),
        in_specs=[pl.BlockSpec((1, s, cw), lambda i, g: (i, 0, g)),
                  pl.BlockSpec((1, s, cw), lambda i, g: (i, 0, nblk + g)),
                  pl.BlockSpec((2 * r, 2 * r), lambda i, g: (0, 0)),
                  pl.BlockSpec((r, r, 2 * r), lambda i, g: (0, 0, 0))],
        out_specs=pl.BlockSpec((1, s, cw), lambda i, g: (i, 0, g)),
        out_shape=jax.ShapeDtypeStruct((b, s, F_WIDTH), BF16),
        scratch_shapes=[pltpu.VMEM((4, r * X_PITCH, LANES), F32),
                        pltpu.VMEM((2, r * Y_PITCH, LANES), F32),
                        pltpu.VMEM((2, r * X_PITCH, LANES), F32)],
        compiler_params=pltpu.CompilerParams(vmem_limit_bytes=VMEM_LIMIT),
        name="fft",
    )(d, d, w1, m3)


def _attn_kernel(qt_ref, kc_ref, kl_ref, vct_ref, vlt_ref, o_ref,
                 s_scr0, s_scr1, p_scr0, p_scr1, acc_scr, m_scr, o_scr, *, nq, sck):
    j = pl.program_id(1)
    tq = qt_ref.shape[3]
    n_ctx = kc_ref.shape[2]
    n_lat = kl_ref.shape[2]
    nh = MLA_HEADS
    n_body = ATTN_BODIES
    lat_rows = n_lat // n_body
    ctx_rows = n_ctx // n_body

    def fold8(v, op):
        return op(v.reshape(v.shape[0] // 8, 8, tq), axis=0)

    s_scr = (s_scr0, s_scr1)
    p_scr = (p_scr0, p_scr1)

    def phase(h, s_qk, blk, do_qk, do_ex, do_pv):
        h_pv = (h + nh - 2) % nh
        s_ex = 1 - s_qk
        if do_ex:
            mb = jnp.broadcast_to(jnp.max(m_scr[s_ex], axis=0, keepdims=True), (8, tq))
        if do_pv:
            acc_scr[...] = jnp.zeros_like(acc_scr)

        def work(k_ref, src, row0, n, m8c):
            rows = pl.ds(row0, n)
            if do_ex:
                sv = s_scr[s_ex][rows, :]
                p = jnp.exp2(sv.reshape(n // 8, 8, tq) - mb[None]).reshape(n, tq)
                p_scr[s_ex][rows, :] = p.astype(BF16)
            if do_qk:
                s = _dot(k_ref[0, h, pl.ds(src, n), :], qt_ref[0, h])
                s_scr[s_qk][rows, :] = s
                m8c = jnp.maximum(m8c, fold8(s, jnp.max))
            return m8c

        def body(c, m8c):
            lat0 = pl.multiple_of(c * lat_rows, lat_rows)
            ctx0 = pl.multiple_of(c * ctx_rows, ctx_rows)
            for sub in range(lat_rows // sck):
                m8c = work(kl_ref, lat0 + sub * sck, lat0 + sub * sck, sck, m8c)
            m8c = work(kc_ref, ctx0, n_lat + ctx0, ctx_rows, m8c)
            if do_pv:
                acc_scr[...] += (
                    _dot(vlt_ref[0, h_pv, :, pl.ds(lat0, lat_rows)], p_scr[s_qk][pl.ds(lat0, lat_rows), :])
                    + _dot(vct_ref[0, h_pv, :, pl.ds(ctx0, ctx_rows)], p_scr[s_qk][pl.ds(n_lat + ctx0, ctx_rows), :]))
            return m8c

        m8 = lax.fori_loop(0, n_body, body, jnp.full((8, tq), -jnp.inf, F32))
        if do_pv:
            pv_blk = jnp.where(jnp.asarray(h) < 2, blk - 1, blk)
            acc = acc_scr[...]
            o_scr[pv_blk % 2, h_pv] = acc[:V_DIM] / acc[V_DIM:V_DIM + 1]
        if do_qk:
            m_scr[s_qk] = m8

    @pl.when(j == 0)
    def _():
        phase(0, 0, 0, True, False, False)
        phase(1, 1, 0, True, True, False)

    @pl.when(j < nq)
    def _():
        def run(pair, carry):
            phase(2 * pair, 0, j, True, True, True)
            phase(2 * pair + 1, 1, j, True, True, True)
            return carry
        lax.fori_loop(jnp.where(j == 0, 1, 0), nh // 2, run, 0)

    @pl.when(j == nq)
    def _():
        phase(0, 0, nq, False, True, True)
        phase(1, 1, nq, False, False, True)

    @pl.when(j > 0)
    def _():
        done = o_scr[(j - 1) % 2]
        o_ref[0] = done.reshape(nh * V_DIM, tq).T.astype(BF16)


def _attention(qt, kc, kl, vct, vlt, tq, sck):
    b, nh, hp, s = qt.shape
    n_ctx = kc.shape[2]
    nq = s // tq
    once = dict(pipeline_mode=pl.Buffered(1))
    k_map = lambda i, j: (jnp.minimum(i + (j == nq).astype(jnp.int32), b - 1), 0, 0, 0)
    return pl.pallas_call(
        functools.partial(_attn_kernel, nq=nq, sck=sck),
        grid=(b, nq + 1),
        in_specs=[pl.BlockSpec((1, nh, hp, tq), lambda i, j: (i, 0, 0, jnp.minimum(j, nq - 1))),
                  pl.BlockSpec((1, nh, n_ctx, hp), k_map),
                  pl.BlockSpec((1, nh, s, hp), k_map),
                  pl.BlockSpec((1, nh, V_PAD, n_ctx), lambda i, j: (i, 0, 0, 0), **once),
                  pl.BlockSpec((1, nh, V_PAD, s), lambda i, j: (i, 0, 0, 0), **once)],
        out_specs=pl.BlockSpec((1, tq, MLA_WIDTH), lambda i, j: (i, jnp.maximum(j - 1, 0), 0)),
        out_shape=jax.ShapeDtypeStruct((b, s, MLA_WIDTH), BF16),
        scratch_shapes=[pltpu.VMEM((n_ctx + s, tq), F32), pltpu.VMEM((n_ctx + s, tq), F32),
                        pltpu.VMEM((n_ctx + s, tq), BF16), pltpu.VMEM((n_ctx + s, tq), BF16),
                        pltpu.VMEM((V_PAD, tq), F32),
                        pltpu.VMEM((2, 8, tq), F32),
                        pltpu.VMEM((2, nh, V_DIM, tq), F32)],
        compiler_params=pltpu.CompilerParams(vmem_limit_bytes=VMEM_LIMIT,
                                             dimension_semantics=("arbitrary", "arbitrary")),
        name="attention",
    )(qt, kc, kl, vct, vlt)


def _merge_kernel(x_ref, mod_ref, attn_ref, z_ref, gates_ref, wf_ref, bf_ref, wo_ref, bo_ref, g_ref, b_ref, o_ref):
    four = _dot(z_ref[0], wf_ref[...]) + bf_ref[...]
    gates = gates_ref[0]
    ya = attn_ref[0].astype(F32) * gates[:, :MLA_WIDTH].astype(F32)
    yf = four * gates[:, MLA_WIDTH:].astype(F32)
    ycat = jnp.concatenate([ya.astype(BF16), yf.astype(BF16)], axis=1)
    y = _dot(ycat, wo_ref[...]) + bo_ref[...]
    r = DEEPNORM_ALPHA * x_ref[0] + mod_ref[0, 2:3, :] * y
    o_ref[0] = _layer_norm_rows(r) * g_ref[...] + b_ref[...]


def _merge(x, mod, attn, z, gates, wf, bf, wo, bo, g, bb, tm):
    b, s, _ = x.shape
    const = lambda *shape: pl.BlockSpec(shape, lambda i, j: (0,) * len(shape))
    tok = lambda w: pl.BlockSpec((1, tm, w), lambda i, j: (i, j, 0))
    return pl.pallas_call(
        _merge_kernel,
        grid=(b, s // tm),
        in_specs=[tok(D_MODEL), pl.BlockSpec((1, 3, D_MODEL), lambda i, j: (i, 0, 0)),
                  tok(MLA_WIDTH), tok(F_WIDTH), tok(D_MIX),
                  const(F_WIDTH, F_WIDTH), const(1, F_WIDTH), const(D_MIX, D_MODEL), const(1, D_MODEL),
                  const(1, D_MODEL), const(1, D_MODEL)],
        out_specs=tok(D_MODEL),
        out_shape=jax.ShapeDtypeStruct((b, s, D_MODEL), x.dtype),
        compiler_params=pltpu.CompilerParams(vmem_limit_bytes=VMEM_LIMIT),
        name="merge",
    )(x, mod, attn, z, gates, wf, bf, wo, bo, g, bb)


def _rot_cols(w):
    half = w.shape[-1] // 2
    return jnp.concatenate([-w[..., half:], w[..., :half]], axis=-1)


def _rope_tables(s):
    n_rows = s // GRID_W
    pos = np.arange(s)
    rows = (pos // GRID_W).astype(np.float64)
    cols = (pos % GRID_W).astype(np.float64)
    assert n_rows * GRID_W == s
    axis_dim = ROPE_DIM // 2
    inv_freq = ROPE_BASE ** (-np.arange(0, axis_dim, 2, dtype=np.float64) / axis_dim)
    ang = np.concatenate([rows[:, None] * inv_freq, cols[:, None] * inv_freq], axis=-1)
    ang = np.concatenate([ang, ang], axis=-1)
    return np.cos(ang), np.sin(ang)


def _dft_constants():
    r = FFT_RADIX
    n = r * r
    k = np.arange(r, dtype=np.float64)
    ang1 = 2.0 * np.pi * np.outer(k, k) / r
    c1, s1 = np.cos(ang1), np.sin(ang1)
    w1 = np.block([[c1, s1], [-s1, c1]])
    k1 = np.arange(r)[:, None, None]
    k2 = np.arange(r)[None, :, None]
    t2 = np.arange(r)[None, None, :]
    ang3 = 2.0 * np.pi * t2 * (k1 + r * k2) / n
    m3 = np.concatenate([np.cos(ang3), np.sin(ang3)], axis=-1) / r
    c = np.arange(F_GROUP_DIM, dtype=np.float64)
    angc = 2.0 * np.pi * np.outer(c, c) / F_GROUP_DIM
    norm = 1.0 / math.sqrt(F_GROUP_DIM)
    eye = np.eye(F_GROUPS)
    wc = np.concatenate([np.kron(eye, np.cos(angc) * norm), -np.kron(eye, np.sin(angc) * norm)], axis=1)
    return w1, m3, wc


def kernel(x, c, ctx, c_ctx, w_ada, b_ada, w_in, b_in, q_norm_g, w_q_up, kv_norm_g, w_kv_up, w_fourier,
           b_fourier, w_out, b_out, post_ln_g, post_ln_b):
    b, s, _ = x.shape
    assert s == FFT_RADIX * FFT_RADIX and w_ada.shape[0] == 1
    l = 0

    wi, bi = w_in[l], b_in[l]
    o_q, o_kv, o_kr, o_gm, o_f, o_gf = 0, Q_LORA, Q_LORA + KV_LORA, Q_LORA + KV_LORA + ROPE_DIM, \
        Q_LORA + KV_LORA + ROPE_DIM + MLA_WIDTH, Q_LORA + KV_LORA + ROPE_DIM + MLA_WIDTH + F_WIDTH

    def in_cols(a):
        kr = a[..., o_kr:o_gm]
        zpad = jnp.zeros(a.shape[:-1] + (LANES - ROPE_DIM,), a.dtype)
        return jnp.concatenate([a[..., o_q:o_kr], kr, zpad, a[..., o_gm:]], axis=-1)

    w_in_p = in_cols(wi.astype(BF16))
    b_in_p = in_cols(bi)[None, :]
    w_ctx = w_in_p[:, COL_CKV:COL_GMLA]
    b_ctx = b_in_p[:, COL_CKV:COL_GMLA]

    wq = w_q_up[l].reshape(Q_LORA, MLA_HEADS, QK_DIM)
    wq_pad = jnp.pad(wq, ((0, 0), (0, 0), (0, HEAD_PAD - QK_DIM))).reshape(Q_LORA, MLA_HEADS * HEAD_PAD)
    wq_rot = _rot_cols(wq[..., NOPE_DIM:]).reshape(Q_LORA, MLA_HEADS * ROPE_DIM)
    wqt = jnp.concatenate([wq_pad, wq_rot], axis=1).T.astype(BF16)

    wkv = w_kv_up[l].reshape(KV_LORA, MLA_HEADS, NOPE_DIM + V_DIM)
    wk_pad = jnp.pad(wkv[..., :NOPE_DIM], ((0, 0), (0, 0), (0, HEAD_PAD - NOPE_DIM)))
    wk_pad = wk_pad.reshape(KV_LORA, MLA_HEADS * HEAD_PAD)
    place = np.zeros((LANES, MLA_HEADS, HEAD_PAD), np.float32)
    for j in range(ROPE_DIM):
        place[j, :, NOPE_DIM + j] = 1.0
    wk2 = jnp.concatenate([wk_pad, jnp.asarray(place.reshape(LANES, -1))], axis=0).astype(BF16)
    wvt = wkv[..., NOPE_DIM:].reshape(KV_LORA, MLA_WIDTH).T.astype(BF16)

    w1_np, m3_np, wc_np = _dft_constants()
    w1 = jnp.asarray(w1_np, F32).astype(BF16)
    m3 = jnp.asarray(m3_np, F32).astype(BF16)
    wc = jnp.asarray(wc_np, F32).astype(BF16)

    cos_np, sin_np = _rope_tables(s)
    scale = QK_SCALE
    cosq = jnp.asarray((cos_np * scale).T, F32)
    sinq = jnp.asarray((sin_np * scale).T, F32)
    kpad = np.zeros((s, LANES - ROPE_DIM))
    cosk = jnp.asarray(np.concatenate([cos_np, kpad], axis=1), F32)
    sink = jnp.asarray(np.concatenate([sin_np, kpad], axis=1), F32)

    qg = q_norm_g[l][None, :]
    kvg = kv_norm_g[l][None, :]

    rows = ((b + 1 + 7) // 8) * 8
    cvec = jnp.concatenate([c, c_ctx[None, :], jnp.zeros((rows - b - 1, D_MODEL), c.dtype)], axis=0)
    mod = _modulation(cvec, w_ada[l], b_ada[l][None, :]).reshape(rows, 3, D_MODEL)

    kc, vct = _ctx_proj(ctx, mod, b, w_ctx, b_ctx, kvg, wk2, wvt)
    qt, kl, vlt, gates, d = _lat_proj(x, mod, w_in_p, b_in_p, qg, wqt, kvg, wk2, wvt, wc,
                                      cosq, sinq, cosk, sink, tm=PROJ_TOKENS)
    z = _fft(d, w1, m3)
    attn = _attention(qt, kc, kl, vct, vlt, tq=ATTN_QUERIES, sck=ATTN_SUB_KEYS)
    return _merge(x, mod, attn, z, gates, w_fourier[l].astype(BF16), b_fourier[l][None, :],
                  w_out[l].astype(BF16), b_out[l][None, :], post_ln_g[l][None, :], post_ln_b[l][None, :], tm=PROJ_TOKENS)
```

```python
import functools
import math

import jax
import jax.numpy as jnp
import numpy as np
from jax import lax
from jax.experimental import pallas as pl
from jax.experimental.pallas import tpu as pltpu

D_MODEL = 1024
GRID_W = 64
MLA_HEADS = 8
NOPE_DIM = 64
ROPE_DIM = 32
V_DIM = 64
QK_DIM = NOPE_DIM + ROPE_DIM
Q_LORA = 256
KV_LORA = 128
MLA_WIDTH = MLA_HEADS * V_DIM
F_GROUPS = 4
F_GROUP_DIM = 128
F_WIDTH = F_GROUPS * F_GROUP_DIM
D_MIX = MLA_WIDTH + F_WIDTH
ROPE_BASE = 10000.0
LN_EPS = 1e-6
DEPTH = 1
DEEPNORM_ALPHA = (2.0 * DEPTH) ** 0.25
QK_SCALE = math.log2(math.e) / math.sqrt(QK_DIM)

LANES = 128
HEAD_PAD = 128
V_PAD = 80
FFT_RADIX = 64
X_PITCH = 72
Y_PITCH = 136
FFT_UNROLL = 16

COL_QLAT = 0
COL_CKV = 256
COL_KROPE = 384
COL_GMLA = 512
COL_FIN = 1024
COL_GF = 1536
D_IN_PAD = 2048

BF16 = jnp.bfloat16
F32 = jnp.float32

V7X_VMEM_BYTES = 64 * 1024 * 1024
VMEM_LIMIT = V7X_VMEM_BYTES - 6 * 1024 * 1024

PROJ_TOKENS = 1024
ATTN_QUERIES = 512
ATTN_SUB_KEYS = 128
ATTN_BODIES = 2
MOD_COLS = 1024
FFT_COPY_UNROLL = 4


def _dot(a, b):
    return jnp.dot(a, b, preferred_element_type=F32)


def _dot_nt(a, b):
    return lax.dot_general(a, b, (((1,), (1,)), ((), ())), preferred_element_type=F32)


def _layer_norm_rows(x):
    mu = jnp.mean(x, axis=-1, keepdims=True)
    xc = x - mu
    var = jnp.mean(xc * xc, axis=-1, keepdims=True)
    return xc * lax.rsqrt(var + LN_EPS)


def _rms_rows(x, g):
    return x * lax.rsqrt(jnp.mean(x * x, axis=-1, keepdims=True) + LN_EPS) * g


def _silu(x):
    return x * jax.nn.sigmoid(x)


def _mod_kernel(c_ref, w_ref, b_ref, o_ref):
    a = _silu(c_ref[...]).astype(BF16)
    o_ref[...] = _dot(a, w_ref[...].astype(BF16)) + b_ref[...]


def _modulation(cvec, w_ada, b_ada):
    rows = cvec.shape[0]
    n = w_ada.shape[1]
    tn = MOD_COLS
    return pl.pallas_call(
        _mod_kernel,
        grid=(n // tn,),
        in_specs=[pl.BlockSpec((rows, D_MODEL), lambda j: (0, 0)),
                  pl.BlockSpec((D_MODEL, tn), lambda j: (0, j)),
                  pl.BlockSpec((1, tn), lambda j: (0, j))],
        out_specs=pl.BlockSpec((rows, tn), lambda j: (0, j)),
        out_shape=jax.ShapeDtypeStruct((rows, n), F32),
        name="modulation",
    )(cvec, w_ada, b_ada)


def _store_heads(k_ref, vt_ref, k_all, vt_all):
    for hd in range(MLA_HEADS):
        k_ref[0, hd] = k_all[:, hd * HEAD_PAD:(hd + 1) * HEAD_PAD].astype(BF16)
        vt_ref[0, hd] = vt_all[hd * V_DIM:(hd + 1) * V_DIM].astype(BF16)


def _ctx_kernel(x_ref, mod_ref, w_ref, b_ref, kvg_ref, wk2_ref, wvt_ref, k_ref, vt_ref):
    y = _layer_norm_rows(x_ref[0])
    h = (y * (1.0 + mod_ref[0, 1:2, :]) + mod_ref[0, 0:1, :]).astype(BF16)
    proj = _dot(h, w_ref[...]) + b_ref[...]
    ckv_n = _rms_rows(proj[:, :KV_LORA], kvg_ref[...])
    ckv_b = ckv_n.astype(BF16)
    kin = jnp.concatenate([ckv_b, proj[:, KV_LORA:].astype(BF16)], axis=1)
    _store_heads(k_ref, vt_ref, _dot(kin, wk2_ref[...]), _dot_nt(wvt_ref[...], ckv_b))


def _ctx_proj(ctx, mod, mod_row, w_ctx, b_ctx, kvg, wk2, wvt):
    b, t, _ = ctx.shape
    const = lambda *shape: pl.BlockSpec(shape, lambda i: (0,) * len(shape))
    return pl.pallas_call(
        _ctx_kernel,
        grid=(b,),
        in_specs=[pl.BlockSpec((1, t, D_MODEL), lambda i: (i, 0, 0)),
                  pl.BlockSpec((1, 3, D_MODEL), lambda i: (mod_row, 0, 0)),
                  const(D_MODEL, 2 * LANES), const(1, 2 * LANES), const(1, KV_LORA),
                  const(2 * LANES, MLA_HEADS * HEAD_PAD), const(MLA_WIDTH, KV_LORA)],
        out_specs=[pl.BlockSpec((1, MLA_HEADS, t, HEAD_PAD), lambda i: (i, 0, 0, 0)),
                   pl.BlockSpec((1, MLA_HEADS, V_DIM, t), lambda i: (i, 0, 0, 0))],
        out_shape=[jax.ShapeDtypeStruct((b, MLA_HEADS, t, HEAD_PAD), BF16),
                   jax.ShapeDtypeStruct((b, MLA_HEADS, V_DIM, t), BF16)],
        compiler_params=pltpu.CompilerParams(vmem_limit_bytes=VMEM_LIMIT),
        name="ctx_proj",
    )(ctx, mod, w_ctx, b_ctx, kvg, wk2, wvt)


def _lat_kernel(x_ref, mod_ref, w_in_ref, b_in_ref, qg_ref, wqt_ref, kvg_ref, wk2_ref, wvt_ref, wc_ref,
                cosq_ref, sinq_ref, cosk_ref, sink_ref,
                qt_ref, k_ref, vt_ref, gates_ref, d_ref, h_scr):
    y = _layer_norm_rows(x_ref[0])
    h_scr[...] = (y * (1.0 + mod_ref[0, 1:2, :]) + mod_ref[0, 0:1, :]).astype(BF16)

    def proj(lo, hi):
        return _dot(h_scr[...], w_in_ref[:, lo:hi]) + b_in_ref[:, lo:hi]

    low = proj(COL_QLAT, COL_GMLA)

    qn = _rms_rows(low[:, COL_QLAT:COL_CKV], qg_ref[...]).astype(BF16)
    qt2 = _dot_nt(wqt_ref[...], qn)
    scale = QK_SCALE
    cosq = cosq_ref[...]
    sinq = sinq_ref[...]
    rot_base = MLA_HEADS * HEAD_PAD
    for hd in range(MLA_HEADS):
        base = hd * HEAD_PAD
        qt_ref[0, hd, :NOPE_DIM, :] = (qt2[base:base + NOPE_DIM] * scale).astype(BF16)
        rope = (qt2[base + NOPE_DIM:base + QK_DIM] * cosq
                + qt2[rot_base + hd * ROPE_DIM:rot_base + (hd + 1) * ROPE_DIM] * sinq)
        qt_ref[0, hd, NOPE_DIM:QK_DIM, :] = rope.astype(BF16)
        qt_ref[0, hd, QK_DIM:, :] = jnp.zeros((HEAD_PAD - QK_DIM, qt2.shape[1]), BF16)

    ckv_b = _rms_rows(low[:, COL_CKV:COL_KROPE], kvg_ref[...]).astype(BF16)
    kr = low[:, COL_KROPE:COL_GMLA]
    half = ROPE_DIM // 2
    rot = pltpu.roll(kr, half, axis=1) - pltpu.roll(kr, LANES - half, axis=1)
    kro = kr * cosk_ref[...] + rot * sink_ref[...]
    kin = jnp.concatenate([ckv_b, kro.astype(BF16)], axis=1)
    _store_heads(k_ref, vt_ref, _dot(kin, wk2_ref[...]), _dot_nt(wvt_ref[...], ckv_b))

    gates_ref[0, :, :MLA_WIDTH] = _silu(proj(COL_GMLA, COL_FIN)).astype(BF16)
    gates_ref[0, :, MLA_WIDTH:] = _silu(proj(COL_GF, D_IN_PAD)).astype(BF16)

    d_ref[0] = _dot(proj(COL_FIN, COL_GF).astype(BF16), wc_ref[...]).astype(BF16)


def _lat_proj(x, mod, w_in_p, b_in_p, qg, wqt, kvg, wk2, wvt, wc, cosq, sinq, cosk, sink, tm):
    b, s, _ = x.shape
    const = lambda *shape: pl.BlockSpec(shape, lambda i, j: (0,) * len(shape))
    hp = MLA_HEADS * HEAD_PAD
    return pl.pallas_call(
        _lat_kernel,
        grid=(b, s // tm),
        in_specs=[pl.BlockSpec((1, tm, D_MODEL), lambda i, j: (i, j, 0)),
                  pl.BlockSpec((1, 3, D_MODEL), lambda i, j: (i, 0, 0)),
                  const(D_MODEL, D_IN_PAD), const(1, D_IN_PAD), const(1, Q_LORA),
                  const(hp + MLA_HEADS * ROPE_DIM, Q_LORA), const(1, KV_LORA),
                  const(2 * LANES, hp), const(MLA_WIDTH, KV_LORA), const(F_WIDTH, 2 * F_WIDTH),
                  pl.BlockSpec((ROPE_DIM, tm), lambda i, j: (0, j)),
                  pl.BlockSpec((ROPE_DIM, tm), lambda i, j: (0, j)),
                  pl.BlockSpec((tm, LANES), lambda i, j: (j, 0)),
                  pl.BlockSpec((tm, LANES), lambda i, j: (j, 0))],
        out_specs=[pl.BlockSpec((1, MLA_HEADS, HEAD_PAD, tm), lambda i, j: (i, 0, 0, j)),
                   pl.BlockSpec((1, MLA_HEADS, tm, HEAD_PAD), lambda i, j: (i, 0, j, 0)),
                   pl.BlockSpec((1, MLA_HEADS, V_DIM, tm), lambda i, j: (i, 0, 0, j)),
                   pl.BlockSpec((1, tm, D_MIX), lambda i, j: (i, j, 0)),
                   pl.BlockSpec((1, tm, 2 * F_WIDTH), lambda i, j: (i, j, 0))],
        out_shape=[jax.ShapeDtypeStruct((b, MLA_HEADS, HEAD_PAD, s), BF16),
                   jax.ShapeDtypeStruct((b, MLA_HEADS, s, HEAD_PAD), BF16),
                   jax.ShapeDtypeStruct((b, MLA_HEADS, V_DIM, s), BF16),
                   jax.ShapeDtypeStruct((b, s, D_MIX), BF16),
                   jax.ShapeDtypeStruct((b, s, 2 * F_WIDTH), BF16)],
        scratch_shapes=[pltpu.VMEM((tm, D_MODEL), BF16)],
        compiler_params=pltpu.CompilerParams(vmem_limit_bytes=VMEM_LIMIT),
        name="lat_proj",
    )(x, mod, w_in_p, b_in_p, qg, wqt, kvg, wk2, wvt, wc, cosq, sinq, cosk, sink)


def _fft_kernel(dr_ref, di_ref, w1_ref, m3_ref, z_ref, xs, ys, zs):
    r = FFT_RADIX

    def fill(g, carry):
        for u in range(FFT_COPY_UNROLL):
            t1 = g * FFT_COPY_UNROLL + u
            src = pl.multiple_of(t1 * r, r)
            dst = pl.multiple_of(t1 * X_PITCH, 8)
            re = dr_ref[0, pl.ds(src, r), :].astype(F32)
            im = di_ref[0, pl.ds(src, r), :].astype(F32)
            xs[0, pl.ds(dst, r), :] = re[:, :LANES]
            xs[1, pl.ds(dst, r), :] = re[:, LANES:]
            xs[2, pl.ds(dst, r), :] = im[:, :LANES]
            xs[3, pl.ds(dst, r), :] = im[:, LANES:]
        return carry

    lax.fori_loop(0, r // FFT_COPY_UNROLL, fill, 0)

    def stage1(g, carry):
        cols = []
        for u in range(FFT_UNROLL):
            t2 = g * FFT_UNROLL + u
            parts = [xs[sl, pl.ds(t2, r, stride=X_PITCH), :] for sl in range(4)]
            cols.append(jnp.concatenate([jnp.concatenate(parts[0:2], axis=1),
                                         jnp.concatenate(parts[2:4], axis=1)], axis=0))
        rhs = jnp.concatenate(cols, axis=1).astype(BF16)
        y = _dot(w1_ref[...], rhs)
        for u in range(FFT_UNROLL):
            dst = pl.multiple_of((g * FFT_UNROLL + u) * Y_PITCH, 8)
            ys[0, pl.ds(dst, 2 * r), :] = y[:, (2 * u) * LANES:(2 * u + 1) * LANES]
            ys[1, pl.ds(dst, 2 * r), :] = y[:, (2 * u + 1) * LANES:(2 * u + 2) * LANES]
        return carry

    lax.fori_loop(0, r // FFT_UNROLL, stage1, 0)

    def stage2(g, carry):
        for u in range(FFT_UNROLL):
            k1 = g * FFT_UNROLL + u
            yr = jnp.concatenate([ys[sl, pl.ds(k1, r, stride=Y_PITCH), :] for sl in range(2)], axis=1)
            yi = jnp.concatenate([ys[sl, pl.ds(r + k1, r, stride=Y_PITCH), :] for sl in range(2)], axis=1)
            rhs = jnp.concatenate([yr, yi], axis=0).astype(BF16)
            z = _dot(m3_ref[k1], rhs)
            zs[0, pl.ds(k1, r, stride=X_PITCH), :] = z[:, :LANES]
            zs[1, pl.ds(k1, r, stride=X_PITCH), :] = z[:, LANES:]
        return carry

    lax.fori_loop(0, r // FFT_UNROLL, stage2, 0)

    def drain(g, carry):
        for u in range(FFT_COPY_UNROLL):
            k2 = g * FFT_COPY_UNROLL + u
            src = pl.multiple_of(k2 * X_PITCH, 8)
            dst = pl.multiple_of(k2 * r, r)
            z_ref[0, pl.ds(dst, r), :] = jnp.concatenate(
                [zs[0, pl.ds(src, r), :], zs[1, pl.ds(src, r), :]], axis=1).astype(BF16)
        return carry

    lax.fori_loop(0, r // FFT_COPY_UNROLL, drain, 0)


def _fft(d, w1, m3):
    b, s, _ = d.shape
    r = FFT_RADIX
    cw = 2 * LANES
    nblk = F_WIDTH // cw
    return pl.pallas_call(
        _fft_kernel,
        grid=(b, nblk),
        in_specs=[pl.BlockSpec((1, s, cw), lambda i, g: (i, 0, g)),
                  pl.BlockSpec((1, s, cw), lambda i, g: (i, 0, nblk + g)),
                  pl.BlockSpec((2 * r, 2 * r), lambda i, g: (0, 0)),
                  pl.BlockSpec((r, r, 2 * r), lambda i, g: (0, 0, 0))],
        out_specs=pl.BlockSpec((1, s, cw), lambda i, g: (i, 0, g)),
        out_shape=jax.ShapeDtypeStruct((b, s, F_WIDTH), BF16),
        scratch_shapes=[pltpu.VMEM((4, r * X_PITCH, LANES), F32),
                        pltpu.VMEM((2, r * Y_PITCH, LANES), F32),
                        pltpu.VMEM((2, r * X_PITCH, LANES), F32)],
        compiler_params=pltpu.CompilerParams(vmem_limit_bytes=VMEM_LIMIT),
        name="fft",
    )(d, d, w1, m3)


def _attn_kernel(qt_ref, kc_ref, kl_ref, vct_ref, vlt_ref, o_ref,
                 s_scr0, s_scr1, p_scr0, p_scr1, acc_scr, m_scr, o_scr, *, nq, sck):
    j = pl.program_id(1)
    tq = qt_ref.shape[3]
    n_ctx = kc_ref.shape[2]
    n_lat = kl_ref.shape[2]
    nh = MLA_HEADS
    n_body = ATTN_BODIES
    lat_rows = n_lat // n_body
    ctx_rows = n_ctx // n_body

    def fold8(v, op):
        return op(v.reshape(v.shape[0] // 8, 8, tq), axis=0)

    def with_ones(v):
        return jnp.concatenate([v, jnp.ones((V_PAD - V_DIM, v.shape[1]), BF16)], axis=0)

    s_scr = (s_scr0, s_scr1)
    p_scr = (p_scr0, p_scr1)

    def phase(h, s_qk, blk, do_qk, do_ex, do_pv):
        h_pv = (h + nh - 2) % nh
        s_ex = 1 - s_qk
        if do_ex:
            mb = jnp.broadcast_to(jnp.max(m_scr[s_ex], axis=0, keepdims=True), (8, tq))
        if do_pv:
            acc_scr[...] = jnp.zeros_like(acc_scr)

        def work(k_ref, src, row0, n, m8c):
            rows = pl.ds(row0, n)
            if do_ex:
                sv = s_scr[s_ex][rows, :]
                p = jnp.exp2(sv.reshape(n // 8, 8, tq) - mb[None]).reshape(n, tq)
                p_scr[s_ex][rows, :] = p.astype(BF16)
            if do_qk:
                s = _dot(k_ref[0, h, pl.ds(src, n), :], qt_ref[0, h])
                s_scr[s_qk][rows, :] = s
                m8c = jnp.maximum(m8c, fold8(s, jnp.max))
            return m8c

        def body(c, m8c):
            lat0 = pl.multiple_of(c * lat_rows, lat_rows)
            ctx0 = pl.multiple_of(c * ctx_rows, ctx_rows)
            for sub in range(lat_rows // sck):
                m8c = work(kl_ref, lat0 + sub * sck, lat0 + sub * sck, sck, m8c)
            m8c = work(kc_ref, ctx0, n_lat + ctx0, ctx_rows, m8c)
            if do_pv:
                acc_scr[...] += (
                    _dot(with_ones(vlt_ref[0, h_pv, :, pl.ds(lat0, lat_rows)]), p_scr[s_qk][pl.ds(lat0, lat_rows), :])
                    + _dot(with_ones(vct_ref[0, h_pv, :, pl.ds(ctx0, ctx_rows)]),
                           p_scr[s_qk][pl.ds(n_lat + ctx0, ctx_rows), :]))
            return m8c

        m8 = lax.fori_loop(0, n_body, body, jnp.full((8, tq), -jnp.inf, F32))
        if do_pv:
            pv_blk = jnp.where(jnp.asarray(h) < 2, blk - 1, blk)
            acc = acc_scr[...]
            o_scr[pv_blk % 2, h_pv] = acc[:V_DIM] / acc[V_DIM:V_DIM + 1]
        if do_qk:
            m_scr[s_qk] = m8

    @pl.when(j == 0)
    def _():
        phase(0, 0, 0, True, False, False)
        phase(1, 1, 0, True, True, False)

    @pl.when(j < nq)
    def _():
        def run(pair, carry):
            phase(2 * pair, 0, j, True, True, True)
            phase(2 * pair + 1, 1, j, True, True, True)
            return carry
        lax.fori_loop(jnp.where(j == 0, 1, 0), nh // 2, run, 0)

    @pl.when(j == nq)
    def _():
        phase(0, 0, nq, False, True, True)
        phase(1, 1, nq, False, False, True)

    @pl.when(j > 0)
    def _():
        done = o_scr[(j - 1) % 2]
        o_ref[0] = done.reshape(nh * V_DIM, tq).T.astype(BF16)


def _attention(qt, kc, kl, vct, vlt, tq, sck):
    b, nh, hp, s = qt.shape
    n_ctx = kc.shape[2]
    nq = s // tq
    k_map = lambda i, j: (jnp.minimum(i + (j == nq).astype(jnp.int32), b - 1), 0, 0, 0)
    return pl.pallas_call(
        functools.partial(_attn_kernel, nq=nq, sck=sck),
        grid=(b, nq + 1),
        in_specs=[pl.BlockSpec((1, nh, hp, tq), lambda i, j: (i, 0, 0, jnp.minimum(j, nq - 1))),
                  pl.BlockSpec((1, nh, n_ctx, hp), k_map),
                  pl.BlockSpec((1, nh, s, hp), k_map),
                  pl.BlockSpec((1, nh, V_DIM, n_ctx), lambda i, j: (i, 0, 0, 0)),
                  pl.BlockSpec((1, nh, V_DIM, s), lambda i, j: (i, 0, 0, 0))],
        out_specs=pl.BlockSpec((1, tq, MLA_WIDTH), lambda i, j: (i, jnp.maximum(j - 1, 0), 0)),
        out_shape=jax.ShapeDtypeStruct((b, s, MLA_WIDTH), BF16),
        scratch_shapes=[pltpu.VMEM((n_ctx + s, tq), F32), pltpu.VMEM((n_ctx + s, tq), F32),
                        pltpu.VMEM((n_ctx + s, tq), BF16), pltpu.VMEM((n_ctx + s, tq), BF16),
                        pltpu.VMEM((V_PAD, tq), F32),
                        pltpu.VMEM((2, 8, tq), F32),
                        pltpu.VMEM((2, nh, V_DIM, tq), F32)],
        compiler_params=pltpu.CompilerParams(vmem_limit_bytes=VMEM_LIMIT,
                                             dimension_semantics=("arbitrary", "arbitrary")),
        name="attention",
    )(qt, kc, kl, vct, vlt)


def _merge_kernel(x_ref, mod_ref, attn_ref, z_ref, gates_ref, wf_ref, bf_ref, wo_ref, bo_ref, g_ref, b_ref, o_ref):
    four = _dot(z_ref[0], wf_ref[...]) + bf_ref[...]
    gates = gates_ref[0]
    ya = attn_ref[0].astype(F32) * gates[:, :MLA_WIDTH].astype(F32)
    yf = four * gates[:, MLA_WIDTH:].astype(F32)
    ycat = jnp.concatenate([ya.astype(BF16), yf.astype(BF16)], axis=1)
    y = _dot(ycat, wo_ref[...]) + bo_ref[...]
    r = DEEPNORM_ALPHA * x_ref[0] + mod_ref[0, 2:3, :] * y
    o_ref[0] = _layer_norm_rows(r) * g_ref[...] + b_ref[...]


def _merge(x, mod, attn, z, gates, wf, bf, wo, bo, g, bb, tm):
    b, s, _ = x.shape
    const = lambda *shape: pl.BlockSpec(shape, lambda i, j: (0,) * len(shape))
    tok = lambda w: pl.BlockSpec((1, tm, w), lambda i, j: (i, j, 0))
    return pl.pallas_call(
        _merge_kernel,
        grid=(b, s // tm),
        in_specs=[tok(D_MODEL), pl.BlockSpec((1, 3, D_MODEL), lambda i, j: (i, 0, 0)),
                  tok(MLA_WIDTH), tok(F_WIDTH), tok(D_MIX),
                  const(F_WIDTH, F_WIDTH), const(1, F_WIDTH), const(D_MIX, D_MODEL), const(1, D_MODEL),
                  const(1, D_MODEL), const(1, D_MODEL)],
        out_specs=tok(D_MODEL),
        out_shape=jax.ShapeDtypeStruct((b, s, D_MODEL), x.dtype),
        compiler_params=pltpu.CompilerParams(vmem_limit_bytes=VMEM_LIMIT),
        name="merge",
    )(x, mod, attn, z, gates, wf, bf, wo, bo, g, bb)


def _rot_cols(w):
    half = w.shape[-1] // 2
    return jnp.concatenate([-w[..., half:], w[..., :half]], axis=-1)


def _rope_tables(s):
    n_rows = s // GRID_W
    pos = np.arange(s)
    rows = (pos // GRID_W).astype(np.float64)
    cols = (pos % GRID_W).astype(np.float64)
    assert n_rows * GRID_W == s
    axis_dim = ROPE_DIM // 2
    inv_freq = ROPE_BASE ** (-np.arange(0, axis_dim, 2, dtype=np.float64) / axis_dim)
    ang = np.concatenate([rows[:, None] * inv_freq, cols[:, None] * inv_freq], axis=-1)
    ang = np.concatenate([ang, ang], axis=-1)
    return np.cos(ang), np.sin(ang)


def _dft_constants():
    r = FFT_RADIX
    n = r * r
    k = np.arange(r, dtype=np.float64)
    ang1 = 2.0 * np.pi * np.outer(k, k) / r
    c1, s1 = np.cos(ang1), np.sin(ang1)
    w1 = np.block([[c1, s1], [-s1, c1]])
    k1 = np.arange(r)[:, None, None]
    k2 = np.arange(r)[None, :, None]
    t2 = np.arange(r)[None, None, :]
    ang3 = 2.0 * np.pi * t2 * (k1 + r * k2) / n
    m3 = np.concatenate([np.cos(ang3), np.sin(ang3)], axis=-1) / r
    c = np.arange(F_GROUP_DIM, dtype=np.float64)
    angc = 2.0 * np.pi * np.outer(c, c) / F_GROUP_DIM
    norm = 1.0 / math.sqrt(F_GROUP_DIM)
    eye = np.eye(F_GROUPS)
    wc = np.concatenate([np.kron(eye, np.cos(angc) * norm), -np.kron(eye, np.sin(angc) * norm)], axis=1)
    return w1, m3, wc


def kernel(x, c, ctx, c_ctx, w_ada, b_ada, w_in, b_in, q_norm_g, w_q_up, kv_norm_g, w_kv_up, w_fourier,
           b_fourier, w_out, b_out, post_ln_g, post_ln_b):
    b, s, _ = x.shape
    assert s == FFT_RADIX * FFT_RADIX and w_ada.shape[0] == 1
    l = 0

    wi, bi = w_in[l], b_in[l]
    o_q, o_kv, o_kr, o_gm, o_f, o_gf = 0, Q_LORA, Q_LORA + KV_LORA, Q_LORA + KV_LORA + ROPE_DIM, \
        Q_LORA + KV_LORA + ROPE_DIM + MLA_WIDTH, Q_LORA + KV_LORA + ROPE_DIM + MLA_WIDTH + F_WIDTH

    def in_cols(a):
        kr = a[..., o_kr:o_gm]
        zpad = jnp.zeros(a.shape[:-1] + (LANES - ROPE_DIM,), a.dtype)
        return jnp.concatenate([a[..., o_q:o_kr], kr, zpad, a[..., o_gm:]], axis=-1)

    w_in_p = in_cols(wi.astype(BF16))
    b_in_p = in_cols(bi)[None, :]
    w_ctx = w_in_p[:, COL_CKV:COL_GMLA]
    b_ctx = b_in_p[:, COL_CKV:COL_GMLA]

    wq = w_q_up[l].reshape(Q_LORA, MLA_HEADS, QK_DIM)
    wq_pad = jnp.pad(wq, ((0, 0), (0, 0), (0, HEAD_PAD - QK_DIM))).reshape(Q_LORA, MLA_HEADS * HEAD_PAD)
    wq_rot = _rot_cols(wq[..., NOPE_DIM:]).reshape(Q_LORA, MLA_HEADS * ROPE_DIM)
    wqt = jnp.concatenate([wq_pad, wq_rot], axis=1).T.astype(BF16)

    wkv = w_kv_up[l].reshape(KV_LORA, MLA_HEADS, NOPE_DIM + V_DIM)
    wk_pad = jnp.pad(wkv[..., :NOPE_DIM], ((0, 0), (0, 0), (0, HEAD_PAD - NOPE_DIM)))
    wk_pad = wk_pad.reshape(KV_LORA, MLA_HEADS * HEAD_PAD)
    place = np.zeros((LANES, MLA_HEADS, HEAD_PAD), np.float32)
    for j in range(ROPE_DIM):
        place[j, :, NOPE_DIM + j] = 1.0
    wk2 = jnp.concatenate([wk_pad, jnp.asarray(place.reshape(LANES, -1))], axis=0).astype(BF16)
    wvt = wkv[..., NOPE_DIM:].reshape(KV_LORA, MLA_WIDTH).T.astype(BF16)

    w1_np, m3_np, wc_np = _dft_constants()
    w1 = jnp.asarray(w1_np, F32).astype(BF16)
    m3 = jnp.asarray(m3_np, F32).astype(BF16)
    wc = jnp.asarray(wc_np, F32).astype(BF16)

    cos_np, sin_np = _rope_tables(s)
    scale = QK_SCALE
    cosq = jnp.asarray((cos_np * scale).T, F32)
    sinq = jnp.asarray((sin_np * scale).T, F32)
    kpad = np.zeros((s, LANES - ROPE_DIM))
    cosk = jnp.asarray(np.concatenate([cos_np, kpad], axis=1), F32)
    sink = jnp.asarray(np.concatenate([sin_np, kpad], axis=1), F32)

    qg = q_norm_g[l][None, :]
    kvg = kv_norm_g[l][None, :]

    rows = ((b + 1 + 7) // 8) * 8
    cvec = jnp.concatenate([c, c_ctx[None, :], jnp.zeros((rows - b - 1, D_MODEL), c.dtype)], axis=0)
    mod = _modulation(cvec, w_ada[l], b_ada[l][None, :]).reshape(rows, 3, D_MODEL)

    kc, vct = _ctx_proj(ctx, mod, b, w_ctx, b_ctx, kvg, wk2, wvt)
    qt, kl, vlt, gates, d = _lat_proj(x, mod, w_in_p, b_in_p, qg, wqt, kvg, wk2, wvt, wc,
                                      cosq, sinq, cosk, sink, tm=PROJ_TOKENS)
    z = _fft(d, w1, m3)
    attn = _attention(qt, kc, kl, vct, vlt, tq=ATTN_QUERIES, sck=ATTN_SUB_KEYS)
    return _merge(x, mod, attn, z, gates, w_fourier[l].astype(BF16), b_fourier[l][None, :],
                  w_out[l].astype(BF16), b_out[l][None, :], post_ln_g[l][None, :], post_ln_b[l][None, :], tm=PROJ_TOKENS)
```
